```python
import jax, jax.numpy as jnp
from jax import lax
import numpy as np

D_MODEL = 2048
BATCH = 4
SEQ = 2048
DEPTH = 4
DEC_BATCH = 8
DEC_SEQ = 1
PAST_LEN = 16384
PAGE_SIZE = 128

N_A_LAYERS = DEPTH // 2
N_B_LAYERS = DEPTH - N_A_LAYERS
CHUNK = 128
D_A = D_MODEL
A_GROUPS = 16
A_GROUP_W = D_A // A_GROUPS
N_HEADS = 16
HEAD_DIM = D_MODEL // N_HEADS
MOBA_BLOCK = 256
MOBA_TOPK = 3
Q_BLOCK = 32
D_FF = -(-8 * D_MODEL // (3 * 256)) * 256
RMS_EPS = 1e-6
LN_EPS = 1e-5
NEG_INF = -1e30
ATTN_SCALE = HEAD_DIM ** -0.5

kernel_name = "yoco_gmlp_moba_decoder_step"


def rmsnorm(x, g):
    x32 = x.astype(jnp.float32)
    y = x32 * lax.rsqrt(jnp.mean(x32 * x32, axis=-1, keepdims=True) + RMS_EPS)
    return (y * g.astype(jnp.float32)).astype(x.dtype)


def layernorm(x, g, b):
    x32 = x.astype(jnp.float32)
    mu = jnp.mean(x32, axis=-1, keepdims=True)
    var = jnp.mean(jnp.square(x32 - mu), axis=-1, keepdims=True)
    y = (x32 - mu) * lax.rsqrt(var + LN_EPS) * g.astype(jnp.float32) + b.astype(jnp.float32)
    return y.astype(x.dtype)


def modulate(h, shift, scale):
    return h * (1 + scale[:, None, :]) + shift[:, None, :]


def swiglu(h, wg, wu, wd):
    return (jax.nn.silu(h @ wg) * (h @ wu)) @ wd


def chunk_spatial_gate(v, w_s, b_s):
    bsz, L, _ = v.shape
    n_ch = -(-L // CHUNK)
    vp = jnp.pad(v, ((0, 0), (0, n_ch * CHUNK - L), (0, 0)))
    vp = vp.reshape(bsz, n_ch, CHUNK, A_GROUPS, A_GROUP_W)
    causal = jnp.tril(jnp.ones((CHUNK, CHUNK), dtype=bool))
    w = jnp.where(causal[None], w_s, jnp.zeros_like(w_s))
    out = jnp.einsum("gts,bcsge->bctge", w, vp) + b_s.T[None, None, :, :, None]
    return out.reshape(bsz, n_ch * CHUNK, D_A)[:, :L]


def gmlp_mixer(h, w_in, b_in, ln_g, ln_b, w_s, b_s, w_out):
    z = jax.nn.gelu(h @ w_in + b_in)
    u, v = jnp.split(z, 2, axis=-1)
    v = layernorm(v, ln_g, ln_b)
    return (u * chunk_spatial_gate(v, w_s, b_s)) @ w_out, v


def prepare_blocks(k_all, v_all):
    bsz, T = k_all.shape[:2]
    nb = -(-T // MOBA_BLOCK)
    pad = ((0, 0), (0, nb * MOBA_BLOCK - T), (0, 0), (0, 0))
    k_blk = jnp.pad(k_all, pad).reshape(bsz, nb, MOBA_BLOCK, N_HEADS, HEAD_DIM)
    v_blk = jnp.pad(v_all, pad).reshape(bsz, nb, MOBA_BLOCK, N_HEADS, HEAD_DIM)
    k_mean = jnp.mean(k_blk.astype(jnp.float32), axis=2).astype(k_all.dtype)
    return k_blk, v_blk, k_mean


def moba_rows(q, q_pos, k_seq, v_seq, k_mean):
    nq = q.shape[0]
    nb = k_seq.shape[0]
    own = q_pos // MOBA_BLOCK
    gate = jnp.einsum("qhd,nhd->hqn", q, k_mean).astype(jnp.float32)
    fully_past = jnp.arange(nb)[None, None, :] < own[None, :, None]
    gate = jnp.where(fully_past, gate, NEG_INF)
    if nb < MOBA_TOPK:
        gate = jnp.pad(gate, ((0, 0), (0, 0), (0, MOBA_TOPK - nb)), constant_values=NEG_INF)
    _, top = lax.top_k(gate, MOBA_TOPK)
    own_b = jnp.broadcast_to(own[None, :, None], (N_HEADS, nq, 1))
    sel_ok = top < own_b
    blk = jnp.concatenate([jnp.minimum(top, nb - 1), own_b], axis=-1)
    h_ix = jnp.arange(N_HEADS)[:, None, None]
    kg = k_seq[blk, :, h_ix]
    vg = v_seq[blk, :, h_ix]
    key_pos = blk[..., None] * MOBA_BLOCK + jnp.arange(MOBA_BLOCK)
    ok = jnp.concatenate([
        jnp.broadcast_to(sel_ok[..., None], (N_HEADS, nq, MOBA_TOPK, MOBA_BLOCK)),
        key_pos[:, :, MOBA_TOPK:, :] <= q_pos[None, :, None, None]], axis=2)
    s = jnp.einsum("qhd,hqjtd->hqjt", q, kg).astype(jnp.float32) * ATTN_SCALE
    s = jnp.where(ok, s, NEG_INF).reshape(N_HEADS, nq, -1)
    p = jax.nn.softmax(s, axis=-1).reshape(kg.shape[:-1]).astype(vg.dtype)
    return jnp.einsum("hqjt,hqjtd->qhd", p, vg)


def moba_attention(q, q_pos, k_blk, v_blk, k_mean):
    bsz, nq = q.shape[:2]
    qb = Q_BLOCK if nq % Q_BLOCK == 0 else nq
    nqb = nq // qb
    q_items = q.reshape(bsz * nqb, qb, N_HEADS, HEAD_DIM)
    pos_items = jnp.tile(q_pos.reshape(nqb, qb), (bsz, 1))
    b_items = jnp.repeat(jnp.arange(bsz, dtype=jnp.int32), nqb)

    def one(item):
        q_i, pos_i, b_i = item
        return moba_rows(q_i, pos_i, k_blk[b_i], v_blk[b_i], k_mean[b_i])

    o = lax.map(one, (q_items, pos_items, b_items))
    return o.reshape(bsz, nq, N_HEADS * HEAD_DIM)


def setup_inputs(seed: int = 0) -> dict:
    key = jax.random.key(seed)
    ks = jax.random.split(key, 40)
    f32 = jnp.float32

    def nrm(k, shape, scale):
        return jax.random.normal(k, shape, f32) * scale

    n_pages = PAST_LEN // PAGE_SIZE
    n_used = DEC_BATCH * n_pages
    n_phys = n_used + (-(-n_used // 4))
    page_table = jax.random.permutation(ks[0], n_phys)[:n_used].reshape(DEC_BATCH, n_pages).astype(jnp.int32)
    ds = D_MODEL ** -0.5
    return {
        "x_prompt": nrm(ks[1], (BATCH, SEQ, D_MODEL), 1.0),
        "x_sample": nrm(ks[2], (DEC_BATCH, DEC_SEQ, D_MODEL), 1.0),
        "c_prompt": nrm(ks[3], (BATCH, D_MODEL), 1.0),
        "c_sample": nrm(ks[4], (DEC_BATCH, D_MODEL), 1.0),
        "cache_k": nrm(ks[5], (n_phys, PAGE_SIZE, N_HEADS, HEAD_DIM), 1.0),
        "cache_v": nrm(ks[6], (n_phys, PAGE_SIZE, N_HEADS, HEAD_DIM), 1.0),
        "page_table": page_table,
        "w_ada": nrm(ks[7], (DEPTH, D_MODEL, 6 * D_MODEL), 0.5 * ds),
        "b_ada": nrm(ks[8], (DEPTH, 6 * D_MODEL), 0.02),
        "g_mix": 1.0 + nrm(ks[9], (DEPTH, D_MODEL), 0.02),
        "g_ffn": 1.0 + nrm(ks[10], (DEPTH, D_MODEL), 0.02),
        "w_a_in": nrm(ks[11], (N_A_LAYERS, D_MODEL, 2 * D_A), ds),
        "b_a_in": nrm(ks[12], (N_A_LAYERS, 2 * D_A), 0.02),
        "ln_a_g": 1.0 + nrm(ks[13], (N_A_LAYERS, D_A), 0.02),
        "ln_a_b": nrm(ks[14], (N_A_LAYERS, D_A), 0.02),
        "w_s": nrm(ks[15], (N_A_LAYERS, A_GROUPS, CHUNK, CHUNK), CHUNK ** -0.5),
        "b_s": 1.0 + nrm(ks[16], (N_A_LAYERS, A_GROUPS, CHUNK), 0.02),
        "w_a_out": nrm(ks[17], (N_A_LAYERS, D_A, D_MODEL), D_A ** -0.5),
        "w_kv_ada": nrm(ks[18], (D_MODEL, 2 * D_MODEL), 0.5 * ds),
        "b_kv_ada": nrm(ks[19], (2 * D_MODEL,), 0.02),
        "g_kv": 1.0 + nrm(ks[20], (D_MODEL,), 0.02),
        "w_k": nrm(ks[21], (D_MODEL, N_HEADS * HEAD_DIM), ds),
        "w_v": nrm(ks[22], (D_MODEL, N_HEADS * HEAD_DIM), ds),
        "w_q": nrm(ks[23], (N_B_LAYERS, D_MODEL, N_HEADS * HEAD_DIM), ds),
        "w_o": nrm(ks[24], (N_B_LAYERS, N_HEADS * HEAD_DIM, D_MODEL), (N_HEADS * HEAD_DIM) ** -0.5),
        "w_ff_gate": nrm(ks[25], (DEPTH, D_MODEL, D_FF), ds),
        "w_ff_up": nrm(ks[26], (DEPTH, D_MODEL, D_FF), ds),
        "w_ff_down": nrm(ks[27], (DEPTH, D_FF, D_MODEL), D_FF ** -0.5),
        "w_fin_ada": nrm(ks[28], (D_MODEL, 2 * D_MODEL), 0.5 * ds),
        "b_fin_ada": nrm(ks[29], (2 * D_MODEL,), 0.02),
        "g_fin": 1.0 + nrm(ks[30], (D_MODEL,), 0.02),
    }


def reference(x_prompt, x_sample, c_prompt, c_sample, cache_k, cache_v, page_table,
              w_ada, b_ada, g_mix, g_ffn, w_a_in, b_a_in, ln_a_g, ln_a_b, w_s, b_s, w_a_out,
              w_kv_ada, b_kv_ada, g_kv, w_k, w_v, w_q, w_o,
              w_ff_gate, w_ff_up, w_ff_down, w_fin_ada, b_fin_ada, g_fin):

    def trunk(x, c, k_past, v_past, pos0):
        bsz, L, _ = x.shape
        q_pos = pos0 + jnp.arange(L, dtype=jnp.int32)
        sc = jax.nn.silu(c)
        a_rows = []
        k_new = v_new = None
        k_blk = v_blk = k_mean = None
        for l in range(DEPTH):
            if l == N_A_LAYERS:
                kv_shift, kv_scale = jnp.split(sc @ w_kv_ada + b_kv_ada, 2, axis=-1)
                h_kv = modulate(rmsnorm(x, g_kv), kv_shift, kv_scale)
                k_new = (h_kv @ w_k).reshape(bsz, L, N_HEADS, HEAD_DIM)
                v_new = (h_kv @ w_v).reshape(bsz, L, N_HEADS, HEAD_DIM)
                k_all = k_new if k_past is None else jnp.concatenate([k_past, k_new], axis=1)
                v_all = v_new if v_past is None else jnp.concatenate([v_past, v_new], axis=1)
                k_blk, v_blk, k_mean = prepare_blocks(k_all, v_all)
            sh1, sc1, g1, sh2, sc2, g2 = jnp.split(sc @ w_ada[l] + b_ada[l], 6, axis=-1)
            h = modulate(rmsnorm(x, g_mix[l]), sh1, sc1)
            if l < N_A_LAYERS:
                o, v_rows = gmlp_mixer(h, w_a_in[l], b_a_in[l], ln_a_g[l], ln_a_b[l],
                                       w_s[l], b_s[l], w_a_out[l])
                a_rows.append(v_rows)
            else:
                j = l - N_A_LAYERS
                q = (h @ w_q[j]).reshape(bsz, L, N_HEADS, HEAD_DIM)
                o = moba_attention(q, q_pos, k_blk, v_blk, k_mean) @ w_o[j]
            x = x + g1[:, None, :] * o
            h = modulate(rmsnorm(x, g_ffn[l]), sh2, sc2)
            x = x + g2[:, None, :] * swiglu(h, w_ff_gate[l], w_ff_up[l], w_ff_down[l])
        f_shift, f_scale = jnp.split(sc @ w_fin_ada + b_fin_ada, 2, axis=-1)
        y = modulate(rmsnorm(x, g_fin), f_shift, f_scale)
        return y, k_new, v_new, a_rows

    y_prompt, k_prompt, v_prompt, _ = trunk(x_prompt, c_prompt, None, None, 0)

    n_seq = page_table.shape[0]
    k_past = cache_k[page_table].reshape(n_seq, -1, N_HEADS, HEAD_DIM)
    v_past = cache_v[page_table].reshape(n_seq, -1, N_HEADS, HEAD_DIM)
    y_sample, k_sample, v_sample, a_rows_s = trunk(x_sample, c_sample, k_past, v_past, PAST_LEN)
    state_a_v_sample = jnp.stack(a_rows_s, axis=0)

    return (y_prompt, y_sample, k_prompt, v_prompt, k_sample, v_sample, state_a_v_sample)
```

```python
import functools

import jax
import jax.numpy as jnp
from jax import lax
from jax.experimental import pallas as pl
from jax.experimental.pallas import tpu as pltpu

F32 = jnp.float32
BF16 = jnp.bfloat16

N_HEADS = 16
HEAD_DIM = 128
CHUNK = 128
A_GROUPS = 16
MOBA_BLOCK = 256
MOBA_TOPK = 3
PAGE_SIZE = 128
PAGES_PER_BLOCK = MOBA_BLOCK // PAGE_SIZE
RMS_EPS = 1e-6
LN_EPS = 1e-5
NEG_INF = -1e30
ATTN_SCALE = HEAD_DIM ** -0.5
GATE_LANES = 128

C_ROWS = 16
MIB = 1024 * 1024
VMEM_LIMIT_V7X = 56 * MIB
COMPILER_TEMP_BYTES = 12 * MIB


def _params(n_axes, vmem_bytes):
    limit = min(int(vmem_bytes) + COMPILER_TEMP_BYTES, VMEM_LIMIT_V7X)
    return pltpu.CompilerParams(dimension_semantics=("arbitrary",) * n_axes,
                                vmem_limit_bytes=limit)


def _nbytes(shape, dtype):
    n = 1
    for s in shape:
        n *= s
    return n * jnp.dtype(dtype).itemsize


def _ada_kernel(c_ref, w_ref, b_ref, o_ref):
    sc = jax.nn.silu(c_ref[...]).astype(BF16)
    o_ref[...] = jnp.dot(sc, w_ref[...].astype(BF16), preferred_element_type=F32) + b_ref[...]


def _ada_call(c_rows, w, b):
    layers, k_dim, n_dim = w.shape
    tn = 1024
    vmem = 2 * (_nbytes((k_dim, tn), F32) + _nbytes((C_ROWS, k_dim), F32) + 2 * _nbytes((C_ROWS, tn), F32))
    return pl.pallas_call(
        _ada_kernel,
        grid=(layers, n_dim // tn),
        in_specs=[pl.BlockSpec((C_ROWS, k_dim), lambda l, n: (0, 0)),
                  pl.BlockSpec((None, k_dim, tn), lambda l, n: (l, 0, n)),
                  pl.BlockSpec((None, 1, tn), lambda l, n: (l, 0, n))],
        out_specs=pl.BlockSpec((None, C_ROWS, tn), lambda l, n: (l, 0, n)),
        out_shape=jax.ShapeDtypeStruct((layers, C_ROWS, n_dim), F32),
        compiler_params=_params(2, vmem),
        name="ada_modulation",
    )(c_rows, w, b)


def _norm_kernel(x_ref, g_ref, sh_ref, sc_ref, o_ref):
    x = x_ref[...]
    y = x * lax.rsqrt(jnp.mean(x * x, axis=-1, keepdims=True) + RMS_EPS) * g_ref[...]
    o_ref[...] = (y * (1.0 + sc_ref[...]) + sh_ref[...]).astype(o_ref.dtype)


def _norm_call(x, g, shift, scale, bm, rows_per_group, out_dtype):
    m_dim, d = x.shape
    r = shift.shape[1]
    blocks_per_group = rows_per_group // bm
    mod_spec = pl.BlockSpec((None, r, d), lambda m: (m // blocks_per_group, 0, 0))
    vmem = 2 * (_nbytes((bm, d), F32) + _nbytes((bm, d), out_dtype) + 3 * _nbytes((8, d), F32)) + 2 * _nbytes((bm, d), F32)
    return pl.pallas_call(
        _norm_kernel,
        grid=(m_dim // bm,),
        in_specs=[pl.BlockSpec((bm, d), lambda m: (m, 0)),
                  pl.BlockSpec((1, d), lambda m: (0, 0)),
                  mod_spec, mod_spec],
        out_specs=pl.BlockSpec((bm, d), lambda m: (m, 0)),
        out_shape=jax.ShapeDtypeStruct((m_dim, d), out_dtype),
        compiler_params=_params(1, vmem),
        name="rmsnorm_modulate",
    )(x, g, shift, scale)


def _mm_kernel(*refs, n_w, epi, k_steps, n_out):
    x_ref = refs[0]
    w_refs = refs[1:1 + n_w]
    pos = 1 + n_w
    if epi == "gelu_bias":
        bias_ref = refs[pos]
        pos += 1
    elif epi == "residual":
        res_ref, gate_ref = refs[pos], refs[pos + 1]
        pos += 2
    out_refs = refs[pos:pos + n_out]
    pos += n_out
    wb_refs = refs[pos:pos + n_w]
    acc_refs = refs[pos + n_w:]
    m = pl.program_id(1)
    k = pl.program_id(2)

    @pl.when(m == 0)
    def _cast_weights():
        for w_ref, wb_ref in zip(w_refs, wb_refs):
            wb_ref[k] = w_ref[...].astype(BF16)

    x = x_ref[...]
    parts = [jnp.dot(x, wb_ref[k], preferred_element_type=F32) for wb_ref in wb_refs]

    def epilogue(accs):
        if epi == "plain":
            for o_ref, a in zip(out_refs, accs):
                o_ref[...] = a.astype(o_ref.dtype)
        elif epi == "gelu_bias":
            out_refs[0][...] = jax.nn.gelu(accs[0] + bias_ref[...]).astype(out_refs[0].dtype)
        elif epi == "swiglu":
            out_refs[0][...] = (jax.nn.silu(accs[0]) * accs[1]).astype(out_refs[0].dtype)
        elif epi == "residual":
            out_refs[0][...] = res_ref[...] + gate_ref[...] * accs[0]

    if k_steps == 1:
        epilogue(parts)
    else:
        @pl.when(k == 0)
        def _first():
            for a_ref, p in zip(acc_refs, parts):
                a_ref[...] = p

        @pl.when(k > 0)
        def _accumulate():
            for a_ref, p in zip(acc_refs, parts):
                a_ref[...] += p

        @pl.when(k == k_steps - 1)
        def _finish():
            epilogue([a_ref[...] for a_ref in acc_refs])


def _mm_call(x, ws, layer, epi, *, bm, bn, k_steps=1, out_dtype=BF16, bias=None, res=None,
             gate=None, rows_per_group=None):
    m_dim, k_dim = x.shape
    n_dim = ws[0].shape[2]
    n_w = len(ws)
    bk = k_dim // k_steps
    grid = (n_dim // bn, m_dim // bm, k_steps)
    last_k = k_steps - 1

    def w_index(n, m, k):
        return (layer, jnp.where(m == 0, k, last_k), n)

    in_specs = [pl.BlockSpec((bm, bk), lambda n, m, k: (m, k))]
    in_specs += [pl.BlockSpec((None, bk, bn), w_index) for _ in ws]
    args = [x] + list(ws)
    vmem = 2 * _nbytes((bm, bk), x.dtype) + n_w * 2 * _nbytes((bk, bn), F32)
    if epi == "gelu_bias":
        in_specs.append(pl.BlockSpec((1, bn), lambda n, m, k: (0, n)))
        args.append(bias)
        vmem += 2 * _nbytes((8, bn), F32)
    elif epi == "residual":
        r = gate.shape[1]
        blocks_per_group = rows_per_group // bm
        in_specs.append(pl.BlockSpec((bm, bn), lambda n, m, k: (m, n)))
        in_specs.append(pl.BlockSpec((None, r, bn), lambda n, m, k: (m // blocks_per_group, 0, n)))
        args += [res, gate]
        vmem += 2 * _nbytes((bm, bn), F32) + 2 * _nbytes((max(r, 8), bn), F32)
    n_out = n_w if epi == "plain" else 1
    out_spec = pl.BlockSpec((bm, bn), lambda n, m, k: (m, n))
    out_sds = jax.ShapeDtypeStruct((m_dim, n_dim), out_dtype)
    vmem += n_out * 2 * _nbytes((bm, bn), out_dtype)
    scratch = [pltpu.VMEM((k_steps, bk, bn), BF16) for _ in ws]
    vmem += n_w * _nbytes((k_steps, bk, bn), BF16)
    if k_steps > 1:
        scratch += [pltpu.VMEM((bm, bn), F32) for _ in ws]
        vmem += n_w * _nbytes((bm, bn), F32)
    outs = pl.pallas_call(
        functools.partial(_mm_kernel, n_w=n_w, epi=epi, k_steps=k_steps, n_out=n_out),
        grid=grid,
        in_specs=in_specs,
        out_specs=[out_spec] * n_out,
        out_shape=[out_sds] * n_out,
        scratch_shapes=scratch,
        compiler_params=_params(3, vmem),
        name="mm_" + epi,
    )(*args)
    return outs if epi == "plain" else outs[0]


def _gmlp_gate_kernel(z_ref, lng_ref, lnb_ref, ws_ref, bst_ref, o_ref, wm_scr, *, rows):
    d_a = lng_ref.shape[1]
    group_w = d_a // A_GROUPS

    @pl.when(pl.program_id(0) == 0)
    def _mask_weights():
        t = lax.broadcasted_iota(jnp.int32, (CHUNK, CHUNK), 0)
        s = lax.broadcasted_iota(jnp.int32, (CHUNK, CHUNK), 1)
        for g in range(A_GROUPS):
            wm_scr[g] = jnp.where(s <= t, ws_ref[g], 0.0).astype(BF16)

    for c in range(rows // CHUNK):
        r0 = c * CHUNK
        v = z_ref[r0:r0 + CHUNK, d_a:]
        mu = jnp.mean(v, axis=-1, keepdims=True)
        var = jnp.mean(jnp.square(v - mu), axis=-1, keepdims=True)
        vln = ((v - mu) * lax.rsqrt(var + LN_EPS) * lng_ref[...] + lnb_ref[...]).astype(BF16)
        for g in range(A_GROUPS):
            c0 = g * group_w
            mixed = jnp.dot(wm_scr[g], vln[:, c0:c0 + group_w], preferred_element_type=F32)
            mixed = mixed + bst_ref[:, g:g + 1]
            o_ref[r0:r0 + CHUNK, c0:c0 + group_w] = (z_ref[r0:r0 + CHUNK, c0:c0 + group_w] * mixed).astype(o_ref.dtype)


def _gmlp_gate_call(z, ln_g, ln_b, w_s, b_s_t, rows):
    m_dim, two_da = z.shape
    d_a = two_da // 2
    vmem = (2 * _nbytes((rows, two_da), F32) + 2 * _nbytes((rows, d_a), BF16)
            + 3 * _nbytes((A_GROUPS, CHUNK, CHUNK), F32) + 4 * _nbytes((8, d_a), F32) + 2 * _nbytes((CHUNK, 128), F32))
    return pl.pallas_call(
        functools.partial(_gmlp_gate_kernel, rows=rows),
        grid=(m_dim // rows,),
        in_specs=[pl.BlockSpec((rows, two_da), lambda m: (m, 0)),
                  pl.BlockSpec((1, d_a), lambda m: (0, 0)),
                  pl.BlockSpec((1, d_a), lambda m: (0, 0)),
                  pl.BlockSpec((A_GROUPS, CHUNK, CHUNK), lambda m: (0, 0, 0)),
                  pl.BlockSpec((CHUNK, A_GROUPS), lambda m: (0, 0))],
        out_specs=pl.BlockSpec((rows, d_a), lambda m: (m, 0)),
        out_shape=jax.ShapeDtypeStruct((m_dim, d_a), BF16),
        scratch_shapes=[pltpu.VMEM((A_GROUPS, CHUNK, CHUNK), BF16)],
        compiler_params=_params(1, vmem),
        name="gmlp_spatial_gate",
    )(z, ln_g, ln_b, w_s, b_s_t)


def _gmlp_gate_row0_kernel(z_ref, lng_ref, lnb_ref, w00_ref, b0_ref, o_ref, v_ref):
    d_a = lng_ref.shape[1]
    v = z_ref[:, d_a:]
    mu = jnp.mean(v, axis=-1, keepdims=True)
    var = jnp.mean(jnp.square(v - mu), axis=-1, keepdims=True)
    vln = (v - mu) * lax.rsqrt(var + LN_EPS) * lng_ref[...] + lnb_ref[...]
    v_ref[...] = vln
    o_ref[...] = (z_ref[:, :d_a] * (w00_ref[...] * vln + b0_ref[...])).astype(o_ref.dtype)


def _gmlp_gate_row0_call(z, ln_g, ln_b, w00, b0):
    s_dim, two_da = z.shape
    d_a = two_da // 2
    full = lambda shape: pl.BlockSpec(shape, lambda i: (0,) * len(shape))
    vmem = 2 * (_nbytes((s_dim, two_da), F32) + 2 * _nbytes((s_dim, d_a), F32) + 4 * _nbytes((8, d_a), F32))
    return pl.pallas_call(
        _gmlp_gate_row0_kernel,
        grid=(1,),
        in_specs=[full((s_dim, two_da)), full((1, d_a)), full((1, d_a)), full((1, d_a)), full((1, d_a))],
        out_specs=[full((s_dim, d_a)), full((s_dim, d_a))],
        out_shape=[jax.ShapeDtypeStruct((s_dim, d_a), BF16), jax.ShapeDtypeStruct((s_dim, d_a), F32)],
        compiler_params=_params(1, vmem),
        name="gmlp_gate_single_token",
    )(z, ln_g, ln_b, w00, b0)


def _moba_kernel(q_ref, k_ref, v_ref, o_ref, kb_scr, vb_scr, km_scr, *, n_blocks):
    i = pl.program_id(2)

    @pl.when(i == 0)
    def _prepare_head():
        kb_scr[...] = k_ref[...].astype(BF16)
        vb_scr[...] = v_ref[...].astype(BF16)
        km_scr[...] = jnp.zeros_like(km_scr)
        for n in range(n_blocks):
            km_scr[n:n + 1, :] = jnp.mean(k_ref[n * MOBA_BLOCK:(n + 1) * MOBA_BLOCK, :], axis=0, keepdims=True)

    q = q_ref[...]
    contract_last = (((1,), (1,)), ((), ()))
    gate = lax.dot_general(q, km_scr[...].astype(BF16), contract_last, preferred_element_type=F32)
    blk = lax.broadcasted_iota(jnp.int32, gate.shape, 1)
    gate = jnp.where(blk < i, gate, NEG_INF)
    sel = jnp.zeros(gate.shape, F32)
    for n in range(n_blocks):
        g_n = gate[:, n:n + 1]
        beats = (gate > g_n) | ((gate == g_n) & (blk < n))
        cnt = jnp.sum(jnp.where(beats, 1.0, 0.0), axis=-1, keepdims=True)
        sel = jnp.where((blk == n) & (cnt < MOBA_TOPK) & (blk < i), 1.0, sel)

    def scores(j):
        start = pl.multiple_of(j * MOBA_BLOCK, MOBA_BLOCK)
        kj = kb_scr[pl.ds(start, MOBA_BLOCK), :]
        vj = vb_scr[pl.ds(start, MOBA_BLOCK), :]
        s = lax.dot_general(q, kj, contract_last, preferred_element_type=F32) * ATTN_SCALE
        return s, vj

    s, vj = scores(i)
    row = lax.broadcasted_iota(jnp.int32, s.shape, 0)
    col = lax.broadcasted_iota(jnp.int32, s.shape, 1)
    s = jnp.where(col <= row, s, NEG_INF)
    m0 = jnp.max(s, axis=-1, keepdims=True)
    p = jnp.exp(s - m0)
    l0 = jnp.sum(p, axis=-1, keepdims=True)
    acc0 = jnp.dot(p.astype(BF16), vj, preferred_element_type=F32)

    def body(j, carry):
        m_old, l_old, acc = carry
        s, vj = scores(j)
        sel_j = jnp.sum(jnp.where(blk == j, sel, 0.0), axis=-1, keepdims=True) > 0.5
        s = jnp.where(sel_j, s, NEG_INF)
        m_new = jnp.maximum(m_old, jnp.max(s, axis=-1, keepdims=True))
        alpha = jnp.exp(m_old - m_new)
        p = jnp.exp(s - m_new)
        l_new = alpha * l_old + jnp.sum(p, axis=-1, keepdims=True)
        acc = alpha * acc + jnp.dot(p.astype(BF16), vj, preferred_element_type=F32)
        return m_new, l_new, acc

    _, l_fin, acc = lax.fori_loop(0, i, body, (m0, l0, acc0))
    o_ref[...] = (acc / l_fin).astype(o_ref.dtype)


def _moba_call(q, k, v):
    b_dim, l_dim, hd = q.shape
    n_blocks = l_dim // MOBA_BLOCK
    q_spec = pl.BlockSpec((None, MOBA_BLOCK, HEAD_DIM), lambda b, h, i: (b, i, h))
    kv_spec = pl.BlockSpec((None, l_dim, HEAD_DIM), lambda b, h, i: (b, 0, h))
    vmem = (4 * _nbytes((l_dim, HEAD_DIM), F32) + 2 * _nbytes((l_dim, HEAD_DIM), BF16)
            + 4 * _nbytes((MOBA_BLOCK, HEAD_DIM), BF16) + _nbytes((GATE_LANES, HEAD_DIM), F32))
    return pl.pallas_call(
        functools.partial(_moba_kernel, n_blocks=n_blocks),
        grid=(b_dim, N_HEADS, n_blocks),
        in_specs=[q_spec, kv_spec, kv_spec],
        out_specs=q_spec,
        out_shape=jax.ShapeDtypeStruct((b_dim, l_dim, hd), BF16),
        scratch_shapes=[pltpu.VMEM((l_dim, HEAD_DIM), BF16), pltpu.VMEM((l_dim, HEAD_DIM), BF16),
                        pltpu.VMEM((GATE_LANES, HEAD_DIM), F32)],
        compiler_params=_params(3, vmem),
        name="moba_prefill_attention",
    )(q, k, v)


def _kmean_kernel(pt_ref, ka_ref, kb_ref, o_ref):
    del pt_ref
    n = pl.program_id(1)
    total = jnp.sum(ka_ref[...], axis=0) + jnp.sum(kb_ref[...], axis=0)
    o_ref[n] = total * (1.0 / MOBA_BLOCK)


def _kmean_call(page_table_flat, cache_k, n_seq, n_blocks):
    pages_per_seq = n_blocks * PAGES_PER_BLOCK
    page_shape = (PAGE_SIZE, N_HEADS, HEAD_DIM)

    def page_spec(half):
        return pl.BlockSpec((None,) + page_shape,
                            lambda b, n, pt: (pt[b * pages_per_seq + PAGES_PER_BLOCK * n + half], 0, 0, 0))

    vmem = 4 * _nbytes(page_shape, F32) + 2 * _nbytes((n_blocks, N_HEADS, HEAD_DIM), F32)
    return pl.pallas_call(
        _kmean_kernel,
        grid_spec=pltpu.PrefetchScalarGridSpec(
            num_scalar_prefetch=1,
            grid=(n_seq, n_blocks),
            in_specs=[page_spec(0), page_spec(1)],
            out_specs=pl.BlockSpec((None, n_blocks, N_HEADS, HEAD_DIM), lambda b, n, pt: (b, 0, 0, 0)),
        ),
        out_shape=jax.ShapeDtypeStruct((n_seq, n_blocks, N_HEADS, HEAD_DIM), F32),
        compiler_params=_params(2, vmem),
        name="paged_k_block_means",
    )(page_table_flat, cache_k, cache_k)


def _topk_kernel(q_ref, km_ref, o_ref):
    n_blocks = km_ref.shape[0]
    km = km_ref[...].reshape(n_blocks * N_HEADS, HEAD_DIM).astype(BF16)
    gate = lax.dot_general(q_ref[...].astype(BF16), km, (((1,), (1,)), ((), ())), preferred_element_type=F32)
    lane = lax.broadcasted_iota(jnp.int32, gate.shape, 1)
    head = lax.broadcasted_iota(jnp.int32, gate.shape, 0)
    gate = jnp.where((lane & (N_HEADS - 1)) == head, gate, -jnp.inf)
    lane_f = lane.astype(F32)
    out_lane = lax.broadcasted_iota(jnp.int32, o_ref.shape, 1)
    out = jnp.zeros(o_ref.shape, jnp.int32)
    for t in range(MOBA_TOPK):
        best = jnp.max(gate, axis=-1, keepdims=True)
        arg = jnp.min(jnp.where(gate == best, lane_f, float(gate.shape[1])), axis=-1, keepdims=True)
        out = jnp.where(out_lane == t, arg.astype(jnp.int32) // N_HEADS, out)
        gate = jnp.where(lane_f == arg, -jnp.inf, gate)
    o_ref[...] = out


def _topk_call(q, kmean):
    s_dim = q.shape[0]
    n_blocks = kmean.shape[1]
    assert N_HEADS & (N_HEADS - 1) == 0
    vmem = 2 * (_nbytes((N_HEADS, HEAD_DIM), F32) + _nbytes((n_blocks, N_HEADS, HEAD_DIM), F32)
                + _nbytes((N_HEADS, GATE_LANES), jnp.int32)) + 8 * _nbytes((N_HEADS, n_blocks * N_HEADS), F32)
    return pl.pallas_call(
        _topk_kernel,
        grid=(s_dim,),
        in_specs=[pl.BlockSpec((None, N_HEADS, HEAD_DIM), lambda b: (b, 0, 0)),
                  pl.BlockSpec((None, n_blocks, N_HEADS, HEAD_DIM), lambda b: (b, 0, 0, 0))],
        out_specs=pl.BlockSpec((None, N_HEADS, GATE_LANES), lambda b: (b, 0, 0)),
        out_shape=jax.ShapeDtypeStruct((s_dim, N_HEADS, GATE_LANES), jnp.int32),
        compiler_params=_params(1, vmem),
        name="moba_decode_topk",
    )(q, kmean)


HEAD_TILE = 8
N_PICKED_PAGES = MOBA_TOPK * PAGES_PER_BLOCK


def _decode_attn_kernel(pt_ref, tk_ref, q_ref, kn_ref, vn_ref, *refs):
    del pt_ref, tk_ref
    k_refs = refs[:N_PICKED_PAGES]
    v_refs = refs[N_PICKED_PAGES:2 * N_PICKED_PAGES]
    o_ref = refs[2 * N_PICKED_PAGES]
    h = pl.program_id(1)
    q = q_ref[pl.ds(h, 1), :]
    s_own = jnp.sum(q * kn_ref[pl.ds(h, 1), :], axis=-1, keepdims=True) * ATTN_SCALE
    q_rows = jnp.broadcast_to(q, (HEAD_TILE, HEAD_DIM)).astype(BF16)
    rows = PAGE_SIZE * HEAD_TILE
    lane = lax.broadcasted_iota(jnp.int32, (HEAD_TILE, rows), 1)
    mine = (lane & (HEAD_TILE - 1)) == (h & (HEAD_TILE - 1))
    scores = []
    for k_ref in k_refs:
        k2 = k_ref[...].reshape(rows, HEAD_DIM).astype(BF16)
        s = lax.dot_general(q_rows, k2, (((1,), (1,)), ((), ())), preferred_element_type=F32) * ATTN_SCALE
        scores.append(jnp.where(mine, s, NEG_INF))
    m = s_own
    for s in scores:
        m = jnp.maximum(m, jnp.max(s, axis=-1, keepdims=True))
    l = jnp.exp(s_own - m)
    acc = l * vn_ref[pl.ds(h, 1), :]
    for s, v_ref in zip(scores, v_refs):
        p = jnp.exp(s - m)
        l = l + jnp.sum(p, axis=-1, keepdims=True)
        acc = acc + jnp.dot(p.astype(BF16), v_ref[...].reshape(rows, HEAD_DIM).astype(BF16), preferred_element_type=F32)
    o_ref[pl.ds(h, 1), :] = (acc / l)[0:1, :]


def _decode_attn_call(page_table_flat, topk_flat, q, k_new, v_new, cache_k, cache_v, pages_per_seq):
    s_dim = q.shape[0]
    row_spec = pl.BlockSpec((None, N_HEADS, HEAD_DIM), lambda b, h, pt, tk: (b, 0, 0))

    def page_spec(j, half):
        def index(b, h, pt, tk):
            block = tk[(b * N_HEADS + h) * MOBA_TOPK + j]
            return (pt[b * pages_per_seq + PAGES_PER_BLOCK * block + half], 0, h // HEAD_TILE, 0)
        return pl.BlockSpec((None, PAGE_SIZE, HEAD_TILE, HEAD_DIM), index)

    page_specs = [page_spec(j, half) for j in range(MOBA_TOPK) for half in range(PAGES_PER_BLOCK)]
    vmem = (4 * N_PICKED_PAGES * _nbytes((PAGE_SIZE, HEAD_TILE, HEAD_DIM), F32) + 8 * _nbytes((N_HEADS, HEAD_DIM), F32)
            + 4 * N_PICKED_PAGES * _nbytes((HEAD_TILE, PAGE_SIZE * HEAD_TILE), F32))
    return pl.pallas_call(
        _decode_attn_kernel,
        grid_spec=pltpu.PrefetchScalarGridSpec(
            num_scalar_prefetch=2,
            grid=(s_dim, N_HEADS),
            in_specs=[row_spec, row_spec, row_spec] + page_specs + page_specs,
            out_specs=row_spec,
        ),
        out_shape=jax.ShapeDtypeStruct((s_dim, N_HEADS, HEAD_DIM), F32),
        compiler_params=_params(2, vmem),
        name="moba_decode_attention",
    )(page_table_flat, topk_flat, q, k_new, v_new, *([cache_k] * N_PICKED_PAGES), *([cache_v] * N_PICKED_PAGES))


def _trunk(x, mods, kv_mod, fin_mod, weights, *, bm, rows_per_group, gate_fn, attn_fn):
    (g_mix, g_ffn, w_a_in, b_a_in, w_a_out, g_kv, w_k, w_v, w_q, w_o,
     w_ff_gate, w_ff_up, w_ff_down, g_fin) = weights
    depth = g_mix.shape[0]
    n_a_layers = w_a_in.shape[0]
    d_ff = w_ff_gate.shape[2]
    norm = functools.partial(_norm_call, bm=min(bm, 512), rows_per_group=rows_per_group)
    mm = functools.partial(_mm_call, bm=bm, bn=512)
    mm_res = functools.partial(mm, epi="residual", out_dtype=F32, rows_per_group=rows_per_group)
    a_rows = []
    k_new = v_new = None
    for l in range(depth):
        if l == n_a_layers:
            kv_shift, kv_scale = kv_mod
            h_kv = norm(x, g_kv, kv_shift, kv_scale, out_dtype=BF16)
            k_new, v_new = mm(h_kv, [w_k, w_v], 0, "plain", out_dtype=F32)
        sh1, sc1, g1, sh2, sc2, g2 = mods[l]
        h = norm(x, g_mix[l:l + 1], sh1, sc1, out_dtype=BF16)
        if l < n_a_layers:
            z = mm(h, [w_a_in], l, "gelu_bias", bias=b_a_in[l:l + 1], out_dtype=F32)
            gated, v_rows = gate_fn(l, z)
            a_rows.append(v_rows)
            x = mm_res(gated, [w_a_out], l, res=x, gate=g1)
        else:
            j = l - n_a_layers
            o = attn_fn(mm(h, [w_q], j, "plain", out_dtype=attn_fn.q_dtype)[0], k_new, v_new)
            x = mm_res(o, [w_o], j, res=x, gate=g1)
        h = norm(x, g_ffn[l:l + 1], sh2, sc2, out_dtype=BF16)
        a = mm(h, [w_ff_gate, w_ff_up], l, "swiglu")
        x = mm_res(a, [w_ff_down], l, res=x, gate=g2, k_steps=2 if d_ff % 512 == 0 else 1)
    f_shift, f_scale = fin_mod
    y = norm(x, g_fin, f_shift, f_scale, out_dtype=F32)
    return y, k_new, v_new, a_rows


def kernel(x_prompt, x_sample, c_prompt, c_sample, cache_k, cache_v, page_table, w_ada, b_ada, g_mix, g_ffn, w_a_in, b_a_in, ln_a_g, ln_a_b, w_s, b_s, w_a_out, w_kv_ada, b_kv_ada, g_kv, w_k, w_v, w_q, w_o, w_ff_gate, w_ff_up, w_ff_down, w_fin_ada, b_fin_ada, g_fin):
    batch, seq, d = x_prompt.shape
    n_seq, dec_seq, _ = x_sample.shape
    assert dec_seq == 1 and batch + n_seq <= C_ROWS
    depth = w_ada.shape[0]
    n_pages, page_size, n_heads, head_dim = cache_k.shape
    assert (page_size, n_heads, head_dim) == (PAGE_SIZE, N_HEADS, HEAD_DIM)
    pages_per_seq = page_table.shape[1]
    n_past_blocks = pages_per_seq // PAGES_PER_BLOCK

    c_rows = jnp.concatenate([c_prompt, c_sample, jnp.zeros((C_ROWS - batch - n_seq, d), F32)], axis=0)
    mods = _ada_call(c_rows, w_ada, b_ada[:, None, :])
    kv_mod = _ada_call(c_rows, w_kv_ada[None], b_kv_ada[None, None, :])[0]
    fin_mod = _ada_call(c_rows, w_fin_ada[None], b_fin_ada[None, None, :])[0]

    def prompt_vecs(rows, n):
        return [rows[:batch, i * d:(i + 1) * d][:, None, :] for i in range(n)]

    def sample_vecs(rows, n):
        return [rows[batch:batch + n_seq, i * d:(i + 1) * d][None] for i in range(n)]

    weights = (g_mix, g_ffn, w_a_in, b_a_in, w_a_out, g_kv[None], w_k[None], w_v[None], w_q, w_o,
               w_ff_gate, w_ff_up, w_ff_down, g_fin[None])
    b_s_t = jnp.swapaxes(b_s, 1, 2)

    def prompt_gate(l, z):
        return _gmlp_gate_call(z, ln_a_g[l:l + 1], ln_a_b[l:l + 1], w_s[l], b_s_t[l], rows=512), None

    def prompt_attn(q, k_new, v_new):
        o = _moba_call(q.reshape(batch, seq, d), k_new.reshape(batch, seq, d), v_new.reshape(batch, seq, d))
        return o.reshape(batch * seq, d)
    prompt_attn.q_dtype = BF16

    y_p, k_p, v_p, _ = _trunk(
        x_prompt.reshape(batch * seq, d),
        [prompt_vecs(mods[l], 6) for l in range(depth)], prompt_vecs(kv_mod, 2), prompt_vecs(fin_mod, 2),
        weights, bm=1024, rows_per_group=seq, gate_fn=prompt_gate, attn_fn=prompt_attn)

    pt_flat = page_table.reshape(-1)
    kmean = _kmean_call(pt_flat, cache_k, n_seq, n_past_blocks)
    group_w = d // A_GROUPS
    heads = (n_seq, N_HEADS, HEAD_DIM)

    def sample_gate(l, z):
        w00 = jnp.repeat(w_s[l, :, 0, 0], group_w)[None, :]
        b0 = jnp.repeat(b_s[l, :, 0], group_w)[None, :]
        return _gmlp_gate_row0_call(z, ln_a_g[l:l + 1], ln_a_b[l:l + 1], w00, b0)

    def sample_attn(q, k_new, v_new):
        q = q.reshape(heads)
        topk = _topk_call(q, kmean)[:, :, :MOBA_TOPK].reshape(-1)
        o = _decode_attn_call(pt_flat, topk, q, k_new.reshape(heads), v_new.reshape(heads),
                              cache_k, cache_v, pages_per_seq)
        return o.reshape(n_seq, d).astype(BF16)
    sample_attn.q_dtype = F32

    y_s, k_s, v_s, a_rows = _trunk(
        x_sample.reshape(n_seq, d),
        [sample_vecs(mods[l], 6) for l in range(depth)], sample_vecs(kv_mod, 2), sample_vecs(fin_mod, 2),
        weights, bm=n_seq, rows_per_group=n_seq, gate_fn=sample_gate, attn_fn=sample_attn)

    state_a_v = jnp.stack(a_rows, axis=0).reshape(len(a_rows), n_seq, 1, d)
    return (y_p.reshape(batch, seq, d), y_s.reshape(n_seq, 1, d),
            k_p.reshape(batch, seq, N_HEADS, HEAD_DIM), v_p.reshape(batch, seq, N_HEADS, HEAD_DIM),
            k_s.reshape(n_seq, 1, N_HEADS, HEAD_DIM), v_s.reshape(n_seq, 1, N_HEADS, HEAD_DIM),
            state_a_v)
```

```python
import functools

import jax
import jax.numpy as jnp
from jax import lax
from jax.experimental import pallas as pl
from jax.experimental.pallas import tpu as pltpu

F32 = jnp.float32
BF16 = jnp.bfloat16

N_HEADS = 16
HEAD_DIM = 128
CHUNK = 128
A_GROUPS = 16
MOBA_BLOCK = 256
MOBA_TOPK = 3
PAGE_SIZE = 128
PAGES_PER_BLOCK = MOBA_BLOCK // PAGE_SIZE
RMS_EPS = 1e-6
LN_EPS = 1e-5
NEG_INF = -1e30
ATTN_SCALE = HEAD_DIM ** -0.5
GATE_LANES = 128

C_ROWS = 16
MIB = 1024 * 1024
VMEM_LIMIT_V7X = 56 * MIB
COMPILER_TEMP_BYTES = 12 * MIB


def _params(n_axes, vmem_bytes):
    limit = min(int(vmem_bytes) + COMPILER_TEMP_BYTES, VMEM_LIMIT_V7X)
    return pltpu.CompilerParams(dimension_semantics=("arbitrary",) * n_axes,
                                vmem_limit_bytes=limit)


def _nbytes(shape, dtype):
    n = 1
    for s in shape:
        n *= s
    return n * jnp.dtype(dtype).itemsize


def _ada_kernel(c_ref, w_ref, b_ref, o_ref):
    sc = jax.nn.silu(c_ref[...]).astype(BF16)
    o_ref[...] = jnp.dot(sc, w_ref[...].astype(BF16), preferred_element_type=F32) + b_ref[...]


def _ada_call(c_rows, w, b):
    layers, k_dim, n_dim = w.shape
    tn = 1024
    vmem = 2 * (_nbytes((k_dim, tn), F32) + _nbytes((C_ROWS, k_dim), F32) + 2 * _nbytes((C_ROWS, tn), F32))
    return pl.pallas_call(
        _ada_kernel,
        grid=(layers, n_dim // tn),
        in_specs=[pl.BlockSpec((C_ROWS, k_dim), lambda l, n: (0, 0)),
                  pl.BlockSpec((None, k_dim, tn), lambda l, n: (l, 0, n)),
                  pl.BlockSpec((None, 1, tn), lambda l, n: (l, 0, n))],
        out_specs=pl.BlockSpec((None, C_ROWS, tn), lambda l, n: (l, 0, n)),
        out_shape=jax.ShapeDtypeStruct((layers, C_ROWS, n_dim), F32),
        compiler_params=_params(2, vmem),
        name="ada_modulation",
    )(c_rows, w, b)


def _norm_kernel(x_ref, g_ref, sh_ref, sc_ref, o_ref):
    x = x_ref[...]
    y = x * lax.rsqrt(jnp.mean(x * x, axis=-1, keepdims=True) + RMS_EPS) * g_ref[...]
    o_ref[...] = (y * (1.0 + sc_ref[...]) + sh_ref[...]).astype(o_ref.dtype)


def _norm_call(x, g, shift, scale, bm, rows_per_group, out_dtype):
    m_dim, d = x.shape
    r = shift.shape[1]
    blocks_per_group = rows_per_group // bm
    mod_spec = pl.BlockSpec((None, r, d), lambda m: (m // blocks_per_group, 0, 0))
    vmem = 2 * (_nbytes((bm, d), F32) + _nbytes((bm, d), out_dtype) + 3 * _nbytes((8, d), F32)) + 2 * _nbytes((bm, d), F32)
    return pl.pallas_call(
        _norm_kernel,
        grid=(m_dim // bm,),
        in_specs=[pl.BlockSpec((bm, d), lambda m: (m, 0)),
                  pl.BlockSpec((1, d), lambda m: (0, 0)),
                  mod_spec, mod_spec],
        out_specs=pl.BlockSpec((bm, d), lambda m: (m, 0)),
        out_shape=jax.ShapeDtypeStruct((m_dim, d), out_dtype),
        compiler_params=_params(1, vmem),
        name="rmsnorm_modulate",
    )(x, g, shift, scale)


def _mm_kernel(*refs, n_w, epi, k_steps, n_out, norm):
    x_ref = refs[0]
    pos = 1
    if norm:
        g_ref, sh_ref, sc_ref = refs[1:4]
        pos = 4
    w_refs = refs[pos:pos + n_w]
    pos += n_w
    if epi == "gelu_bias":
        bias_ref = refs[pos]
        pos += 1
    elif epi == "residual":
        res_ref, gate_ref = refs[pos], refs[pos + 1]
        pos += 2
    out_refs = refs[pos:pos + n_out]
    pos += n_out
    wb_refs = refs[pos:pos + n_w]
    acc_refs = refs[pos + n_w:]
    m = pl.program_id(1)
    k = pl.program_id(2)

    @pl.when(m == 0)
    def _cast_weights():
        for w_ref, wb_ref in zip(w_refs, wb_refs):
            wb_ref[k] = w_ref[...].astype(BF16)

    x = x_ref[...]
    if norm:
        y = x * lax.rsqrt(jnp.mean(x * x, axis=-1, keepdims=True) + RMS_EPS) * g_ref[...]
        x = (y * (1.0 + sc_ref[...]) + sh_ref[...]).astype(BF16)
    parts = [jnp.dot(x, wb_ref[k], preferred_element_type=F32) for wb_ref in wb_refs]

    def epilogue(accs):
        if epi == "plain":
            for o_ref, a in zip(out_refs, accs):
                o_ref[...] = a.astype(o_ref.dtype)
        elif epi == "gelu_bias":
            out_refs[0][...] = jax.nn.gelu(accs[0] + bias_ref[...]).astype(out_refs[0].dtype)
        elif epi == "swiglu":
            out_refs[0][...] = (jax.nn.silu(accs[0]) * accs[1]).astype(out_refs[0].dtype)
        elif epi == "residual":
            out_refs[0][...] = res_ref[...] + gate_ref[...] * accs[0]

    if k_steps == 1:
        epilogue(parts)
    else:
        @pl.when(k == 0)
        def _first():
            for a_ref, p in zip(acc_refs, parts):
                a_ref[...] = p

        @pl.when(k > 0)
        def _accumulate():
            for a_ref, p in zip(acc_refs, parts):
                a_ref[...] += p

        @pl.when(k == k_steps - 1)
        def _finish():
            epilogue([a_ref[...] for a_ref in acc_refs])


def _mm_call(x, ws, layer, epi, *, bm, bn, rows_per_group, k_steps=1, out_dtype=BF16, norm=None, bias=None,
             res=None, gate=None):
    m_dim, k_dim = x.shape
    n_dim = ws[0].shape[2]
    n_w = len(ws)
    bk = k_dim // k_steps
    grid = (n_dim // bn, m_dim // bm, k_steps)
    last_k = k_steps - 1
    blocks_per_group = rows_per_group // bm
    assert norm is None or k_steps == 1

    def w_index(n, m, k):
        return (layer, jnp.where(m == 0, k, last_k), n)

    w_buffers = 1 if k_steps == 1 else 2
    in_specs = [pl.BlockSpec((bm, bk), lambda n, m, k: (m, k))]
    args = [x]
    vmem = 2 * _nbytes((bm, bk), x.dtype) + n_w * w_buffers * _nbytes((bk, bn), F32)
    if norm is not None:
        g, shift, scale = norm
        r = shift.shape[1]
        vec_spec = pl.BlockSpec((None, r, k_dim), lambda n, m, k: (m // blocks_per_group, 0, 0))
        in_specs += [pl.BlockSpec((1, k_dim), lambda n, m, k: (0, 0)), vec_spec, vec_spec]
        args += [g, shift, scale]
        vmem += 6 * _nbytes((max(r, 8), k_dim), F32) + _nbytes((bm, bk), F32) + _nbytes((bm, bk), BF16)
    in_specs += [pl.BlockSpec((None, bk, bn), w_index, pipeline_mode=pl.Buffered(w_buffers)) for _ in ws]
    args += list(ws)
    if epi == "gelu_bias":
        in_specs.append(pl.BlockSpec((1, bn), lambda n, m, k: (0, n)))
        args.append(bias)
        vmem += 2 * _nbytes((8, bn), F32)
    elif epi == "residual":
        r = gate.shape[1]
        in_specs.append(pl.BlockSpec((bm, bn), lambda n, m, k: (m, n)))
        in_specs.append(pl.BlockSpec((None, r, bn), lambda n, m, k: (m // blocks_per_group, 0, n)))
        args += [res, gate]
        vmem += 2 * _nbytes((bm, bn), F32) + 2 * _nbytes((max(r, 8), bn), F32)
    n_out = n_w if epi == "plain" else 1
    out_spec = pl.BlockSpec((bm, bn), lambda n, m, k: (m, n))
    out_sds = jax.ShapeDtypeStruct((m_dim, n_dim), out_dtype)
    vmem += n_out * 2 * _nbytes((bm, bn), out_dtype)
    scratch = [pltpu.VMEM((k_steps, bk, bn), BF16) for _ in ws]
    vmem += n_w * _nbytes((k_steps, bk, bn), BF16)
    if k_steps > 1:
        scratch += [pltpu.VMEM((bm, bn), F32) for _ in ws]
        vmem += n_w * _nbytes((bm, bn), F32)
    outs = pl.pallas_call(
        functools.partial(_mm_kernel, n_w=n_w, epi=epi, k_steps=k_steps, n_out=n_out, norm=norm is not None),
        grid=grid,
        in_specs=in_specs,
        out_specs=[out_spec] * n_out,
        out_shape=[out_sds] * n_out,
        scratch_shapes=scratch,
        compiler_params=_params(3, vmem),
        name="mm_" + epi,
    )(*args)
    return outs if epi == "plain" else outs[0]


def _gmlp_gate_kernel(z_ref, lng_ref, lnb_ref, ws_ref, bst_ref, o_ref, wm_scr, *, rows):
    d_a = lng_ref.shape[1]
    group_w = d_a // A_GROUPS

    @pl.when(pl.program_id(0) == 0)
    def _mask_weights():
        t = lax.broadcasted_iota(jnp.int32, (CHUNK, CHUNK), 0)
        s = lax.broadcasted_iota(jnp.int32, (CHUNK, CHUNK), 1)
        for g in range(A_GROUPS):
            wm_scr[g] = jnp.where(s <= t, ws_ref[g], 0.0).astype(BF16)

    for c in range(rows // CHUNK):
        r0 = c * CHUNK
        v = z_ref[r0:r0 + CHUNK, d_a:]
        mu = jnp.mean(v, axis=-1, keepdims=True)
        var = jnp.mean(jnp.square(v - mu), axis=-1, keepdims=True)
        vln = ((v - mu) * lax.rsqrt(var + LN_EPS) * lng_ref[...] + lnb_ref[...]).astype(BF16)
        for g in range(A_GROUPS):
            c0 = g * group_w
            mixed = jnp.dot(wm_scr[g], vln[:, c0:c0 + group_w], preferred_element_type=F32)
            mixed = mixed + bst_ref[:, g:g + 1]
            o_ref[r0:r0 + CHUNK, c0:c0 + group_w] = (z_ref[r0:r0 + CHUNK, c0:c0 + group_w] * mixed).astype(o_ref.dtype)


def _gmlp_gate_call(z, ln_g, ln_b, w_s, b_s_t, rows):
    m_dim, two_da = z.shape
    d_a = two_da // 2
    vmem = (2 * _nbytes((rows, two_da), F32) + 2 * _nbytes((rows, d_a), BF16)
            + 3 * _nbytes((A_GROUPS, CHUNK, CHUNK), F32) + 4 * _nbytes((8, d_a), F32) + 2 * _nbytes((CHUNK, 128), F32))
    return pl.pallas_call(
        functools.partial(_gmlp_gate_kernel, rows=rows),
        grid=(m_dim // rows,),
        in_specs=[pl.BlockSpec((rows, two_da), lambda m: (m, 0)),
                  pl.BlockSpec((1, d_a), lambda m: (0, 0)),
                  pl.BlockSpec((1, d_a), lambda m: (0, 0)),
                  pl.BlockSpec((A_GROUPS, CHUNK, CHUNK), lambda m: (0, 0, 0)),
                  pl.BlockSpec((CHUNK, A_GROUPS), lambda m: (0, 0))],
        out_specs=pl.BlockSpec((rows, d_a), lambda m: (m, 0)),
        out_shape=jax.ShapeDtypeStruct((m_dim, d_a), BF16),
        scratch_shapes=[pltpu.VMEM((A_GROUPS, CHUNK, CHUNK), BF16)],
        compiler_params=_params(1, vmem),
        name="gmlp_spatial_gate",
    )(z, ln_g, ln_b, w_s, b_s_t)


def _gmlp_gate_row0_kernel(z_ref, lng_ref, lnb_ref, w00_ref, b0_ref, o_ref, v_ref):
    d_a = lng_ref.shape[1]
    v = z_ref[:, d_a:]
    mu = jnp.mean(v, axis=-1, keepdims=True)
    var = jnp.mean(jnp.square(v - mu), axis=-1, keepdims=True)
    vln = (v - mu) * lax.rsqrt(var + LN_EPS) * lng_ref[...] + lnb_ref[...]
    v_ref[...] = vln
    o_ref[...] = (z_ref[:, :d_a] * (w00_ref[...] * vln + b0_ref[...])).astype(o_ref.dtype)


def _gmlp_gate_row0_call(z, ln_g, ln_b, w00, b0):
    s_dim, two_da = z.shape
    d_a = two_da // 2
    full = lambda shape: pl.BlockSpec(shape, lambda i: (0,) * len(shape))
    vmem = 2 * (_nbytes((s_dim, two_da), F32) + 2 * _nbytes((s_dim, d_a), F32) + 4 * _nbytes((8, d_a), F32))
    return pl.pallas_call(
        _gmlp_gate_row0_kernel,
        grid=(1,),
        in_specs=[full((s_dim, two_da)), full((1, d_a)), full((1, d_a)), full((1, d_a)), full((1, d_a))],
        out_specs=[full((s_dim, d_a)), full((s_dim, d_a))],
        out_shape=[jax.ShapeDtypeStruct((s_dim, d_a), BF16), jax.ShapeDtypeStruct((s_dim, d_a), F32)],
        compiler_params=_params(1, vmem),
        name="gmlp_gate_single_token",
    )(z, ln_g, ln_b, w00, b0)


GATE_ROWS = 16
LOG2_E = 1.4426950408889634


def _moba_kernel(q_ref, k_ref, v_ref, o_ref, kb_scr, vt_scr, km_scr, p_scr, *, n_blocks):
    contract_last = (((1,), (1,)), ((), ()))
    kb_scr[...] = k_ref[...].astype(BF16)
    vt_scr[...] = v_ref[...].T.astype(BF16)
    km_scr[...] = jnp.zeros_like(km_scr)
    for n in range(n_blocks):
        km_scr[n:n + 1, :] = jnp.mean(k_ref[n * MOBA_BLOCK:(n + 1) * MOBA_BLOCK, :], axis=0, keepdims=True)
    gate_all = lax.dot_general(km_scr[...].astype(BF16), q_ref[...], contract_last,
                               preferred_element_type=F32)
    blk = lax.broadcasted_iota(jnp.int32, (GATE_ROWS, MOBA_BLOCK), 0)
    key = lax.broadcasted_iota(jnp.int32, (MOBA_BLOCK, MOBA_BLOCK), 0)
    qry = lax.broadcasted_iota(jnp.int32, (MOBA_BLOCK, MOBA_BLOCK), 1)

    for i in range(n_blocks):
        rows = slice(i * MOBA_BLOCK, (i + 1) * MOBA_BLOCK)
        q_t = q_ref[rows, :]
        bias = [None] * i
        if i > MOBA_TOPK:
            gate = jnp.where(blk < i, gate_all[:, rows], NEG_INF)
            for n in range(i):
                g_n = gate[n:n + 1, :]
                beats = (gate > g_n) | ((gate == g_n) & (blk < n))
                cnt = jnp.sum(jnp.where(beats, 1.0, 0.0), axis=0, keepdims=True)
                bias[n] = jnp.where(cnt < MOBA_TOPK, 0.0, NEG_INF)
        kv_len = (i + 1) * MOBA_BLOCK
        s_all = lax.dot_general(kb_scr[0:kv_len, :], q_t, contract_last,
                                preferred_element_type=F32) * (ATTN_SCALE * LOG2_E)
        s_blocks = []
        for j in range(i + 1):
            s = s_all[j * MOBA_BLOCK:(j + 1) * MOBA_BLOCK, :]
            if j == i:
                s = jnp.where(key <= qry, s, NEG_INF)
            elif bias[j] is not None:
                s = s + bias[j]
            s_blocks.append(s)
        m = jnp.max(functools.reduce(jnp.maximum, s_blocks), axis=0, keepdims=True)
        l = None
        for j, s in enumerate(s_blocks):
            p = jnp.exp2(s - m)
            p_scr[j * MOBA_BLOCK:(j + 1) * MOBA_BLOCK, :] = p.astype(BF16)
            p_sum = jnp.sum(p, axis=0, keepdims=True)
            l = p_sum if l is None else l + p_sum
        acc = jnp.dot(vt_scr[:, 0:kv_len], p_scr[0:kv_len, :], preferred_element_type=F32)
        o_ref[rows, :] = (acc / l).T.astype(o_ref.dtype)


def _moba_call(q, k, v):
    b_dim, l_dim, hd = q.shape
    n_blocks = l_dim // MOBA_BLOCK
    assert n_blocks <= GATE_ROWS
    head_spec = pl.BlockSpec((None, l_dim, HEAD_DIM), lambda b, h: (b, 0, h))
    vmem = (4 * _nbytes((l_dim, HEAD_DIM), F32) + 6 * _nbytes((l_dim, HEAD_DIM), BF16)
            + _nbytes((GATE_ROWS, HEAD_DIM), F32) + _nbytes((l_dim, MOBA_BLOCK), BF16)
            + 2 * _nbytes((l_dim, MOBA_BLOCK), F32))
    return pl.pallas_call(
        functools.partial(_moba_kernel, n_blocks=n_blocks),
        grid=(b_dim, N_HEADS),
        in_specs=[head_spec, head_spec, head_spec],
        out_specs=head_spec,
        out_shape=jax.ShapeDtypeStruct((b_dim, l_dim, hd), BF16),
        scratch_shapes=[pltpu.VMEM((l_dim, HEAD_DIM), BF16), pltpu.VMEM((HEAD_DIM, l_dim), BF16),
                        pltpu.VMEM((GATE_ROWS, HEAD_DIM), F32), pltpu.VMEM((l_dim, MOBA_BLOCK), BF16)],
        compiler_params=_params(2, vmem),
        name="moba_prefill_attention",
    )(q, k, v)


KMEAN_BLOCKS_PER_STEP = 4


def _kmean_kernel(pt_ref, *refs):
    del pt_ref
    page_refs, o_ref = refs[:-1], refs[-1]
    first = pl.program_id(1) * KMEAN_BLOCKS_PER_STEP
    for t in range(KMEAN_BLOCKS_PER_STEP):
        pages = page_refs[t * PAGES_PER_BLOCK:(t + 1) * PAGES_PER_BLOCK]
        total = functools.reduce(jnp.add, [jnp.sum(p[...], axis=0) for p in pages])
        o_ref[first + t] = total * (1.0 / MOBA_BLOCK)


def _kmean_call(page_table_flat, cache_k, n_seq, n_blocks):
    pages_per_seq = n_blocks * PAGES_PER_BLOCK
    pages_per_step = KMEAN_BLOCKS_PER_STEP * PAGES_PER_BLOCK
    page_shape = (PAGE_SIZE, N_HEADS, HEAD_DIM)

    def page_spec(slot):
        return pl.BlockSpec((None,) + page_shape,
                            lambda b, n, pt: (pt[b * pages_per_seq + pages_per_step * n + slot], 0, 0, 0))

    vmem = 2 * pages_per_step * _nbytes(page_shape, F32) + 2 * _nbytes((n_blocks, N_HEADS, HEAD_DIM), F32)
    return pl.pallas_call(
        _kmean_kernel,
        grid_spec=pltpu.PrefetchScalarGridSpec(
            num_scalar_prefetch=1,
            grid=(n_seq, n_blocks // KMEAN_BLOCKS_PER_STEP),
            in_specs=[page_spec(slot) for slot in range(pages_per_step)],
            out_specs=pl.BlockSpec((None, n_blocks, N_HEADS, HEAD_DIM), lambda b, n, pt: (b, 0, 0, 0)),
        ),
        out_shape=jax.ShapeDtypeStruct((n_seq, n_blocks, N_HEADS, HEAD_DIM), F32),
        compiler_params=_params(2, vmem),
        name="paged_k_block_means",
    )(page_table_flat, *([cache_k] * pages_per_step))


def _topk_kernel(q_ref, km_ref, o_ref):
    n_blocks = km_ref.shape[0]
    km = km_ref[...].reshape(n_blocks * N_HEADS, HEAD_DIM).astype(BF16)
    gate = lax.dot_general(q_ref[...].astype(BF16), km, (((1,), (1,)), ((), ())), preferred_element_type=F32)
    lane = lax.broadcasted_iota(jnp.int32, gate.shape, 1)
    head = lax.broadcasted_iota(jnp.int32, gate.shape, 0)
    gate = jnp.where((lane & (N_HEADS - 1)) == head, gate, -jnp.inf)
    lane_f = lane.astype(F32)
    out_lane = lax.broadcasted_iota(jnp.int32, o_ref.shape, 1)
    out = jnp.zeros(o_ref.shape, jnp.int32)
    for t in range(MOBA_TOPK):
        best = jnp.max(gate, axis=-1, keepdims=True)
        arg = jnp.min(jnp.where(gate == best, lane_f, float(gate.shape[1])), axis=-1, keepdims=True)
        out = jnp.where(out_lane == t, arg.astype(jnp.int32) // N_HEADS, out)
        gate = jnp.where(lane_f == arg, -jnp.inf, gate)
    o_ref[...] = out


def _topk_call(q, kmean):
    s_dim = q.shape[0]
    n_blocks = kmean.shape[1]
    assert N_HEADS & (N_HEADS - 1) == 0
    vmem = 2 * (_nbytes((N_HEADS, HEAD_DIM), F32) + _nbytes((n_blocks, N_HEADS, HEAD_DIM), F32)
                + _nbytes((N_HEADS, GATE_LANES), jnp.int32)) + 8 * _nbytes((N_HEADS, n_blocks * N_HEADS), F32)
    return pl.pallas_call(
        _topk_kernel,
        grid=(s_dim,),
        in_specs=[pl.BlockSpec((None, N_HEADS, HEAD_DIM), lambda b: (b, 0, 0)),
                  pl.BlockSpec((None, n_blocks, N_HEADS, HEAD_DIM), lambda b: (b, 0, 0, 0))],
        out_specs=pl.BlockSpec((None, N_HEADS, GATE_LANES), lambda b: (b, 0, 0)),
        out_shape=jax.ShapeDtypeStruct((s_dim, N_HEADS, GATE_LANES), jnp.int32),
        compiler_params=_params(1, vmem),
        name="moba_decode_topk",
    )(q, kmean)


HEAD_TILE = 8
N_PICKED_PAGES = MOBA_TOPK * PAGES_PER_BLOCK


def _decode_attn_kernel(pt_ref, tk_ref, q_ref, kn_ref, vn_ref, *refs):
    del pt_ref, tk_ref
    k_refs = refs[:N_PICKED_PAGES]
    v_refs = refs[N_PICKED_PAGES:2 * N_PICKED_PAGES]
    o_ref = refs[2 * N_PICKED_PAGES]
    h = pl.program_id(1)
    q = q_ref[pl.ds(h, 1), :]
    s_own = jnp.sum(q * kn_ref[pl.ds(h, 1), :], axis=-1, keepdims=True) * ATTN_SCALE
    q_rows = jnp.broadcast_to(q, (HEAD_TILE, HEAD_DIM)).astype(BF16)
    rows = PAGE_SIZE * HEAD_TILE
    lane = lax.broadcasted_iota(jnp.int32, (HEAD_TILE, rows), 1)
    mine = (lane & (HEAD_TILE - 1)) == (h & (HEAD_TILE - 1))
    scores = []
    for k_ref in k_refs:
        k2 = k_ref[...].reshape(rows, HEAD_DIM).astype(BF16)
        s = lax.dot_general(q_rows, k2, (((1,), (1,)), ((), ())), preferred_element_type=F32) * ATTN_SCALE
        scores.append(jnp.where(mine, s, NEG_INF))
    m = s_own
    for s in scores:
        m = jnp.maximum(m, jnp.max(s, axis=-1, keepdims=True))
    l = jnp.exp(s_own - m)
    acc = l * vn_ref[pl.ds(h, 1), :]
    for s, v_ref in zip(scores, v_refs):
        p = jnp.exp(s - m)
        l = l + jnp.sum(p, axis=-1, keepdims=True)
        acc = acc + jnp.dot(p.astype(BF16), v_ref[...].reshape(rows, HEAD_DIM).astype(BF16), preferred_element_type=F32)
    o_ref[pl.ds(h, 1), :] = (acc / l)[0:1, :]


def _decode_attn_call(page_table_flat, topk_flat, q, k_new, v_new, cache_k, cache_v, pages_per_seq):
    s_dim = q.shape[0]
    row_spec = pl.BlockSpec((None, N_HEADS, HEAD_DIM), lambda b, h, pt, tk: (b, 0, 0))

    def page_spec(j, half):
        def index(b, h, pt, tk):
            block = tk[(b * N_HEADS + h) * MOBA_TOPK + j]
            return (pt[b * pages_per_seq + PAGES_PER_BLOCK * block + half], 0, h // HEAD_TILE, 0)
        return pl.BlockSpec((None, PAGE_SIZE, HEAD_TILE, HEAD_DIM), index)

    page_specs = [page_spec(j, half) for j in range(MOBA_TOPK) for half in range(PAGES_PER_BLOCK)]
    vmem = (4 * N_PICKED_PAGES * _nbytes((PAGE_SIZE, HEAD_TILE, HEAD_DIM), F32) + 8 * _nbytes((N_HEADS, HEAD_DIM), F32)
            + 4 * N_PICKED_PAGES * _nbytes((HEAD_TILE, PAGE_SIZE * HEAD_TILE), F32))
    return pl.pallas_call(
        _decode_attn_kernel,
        grid_spec=pltpu.PrefetchScalarGridSpec(
            num_scalar_prefetch=2,
            grid=(s_dim, N_HEADS),
            in_specs=[row_spec, row_spec, row_spec] + page_specs + page_specs,
            out_specs=row_spec,
        ),
        out_shape=jax.ShapeDtypeStruct((s_dim, N_HEADS, HEAD_DIM), F32),
        compiler_params=_params(2, vmem),
        name="moba_decode_attention",
    )(page_table_flat, topk_flat, q, k_new, v_new, *([cache_k] * N_PICKED_PAGES), *([cache_v] * N_PICKED_PAGES))


def _trunk(x, mods, kv_mod, fin_mod, weights, *, bm, rows_per_group, gate_fn, attn_fn):
    (g_mix, g_ffn, w_a_in, b_a_in, w_a_out, g_kv, w_k, w_v, w_q, w_o,
     w_ff_gate, w_ff_up, w_ff_down, g_fin) = weights
    depth = g_mix.shape[0]
    n_a_layers = w_a_in.shape[0]
    d_ff = w_ff_gate.shape[2]
    mm = functools.partial(_mm_call, bm=bm, rows_per_group=rows_per_group)
    mm_res = functools.partial(mm, epi="residual", out_dtype=F32)
    wide, narrow = 1024, 512
    a_rows = []
    k_new = v_new = None
    for l in range(depth):
        if l == n_a_layers:
            kv_shift, kv_scale = kv_mod
            k_new, v_new = mm(x, [w_k, w_v], 0, "plain", bn=narrow, out_dtype=F32, norm=(g_kv, kv_shift, kv_scale))
        sh1, sc1, g1, sh2, sc2, g2 = mods[l]
        mix_norm = (g_mix[l:l + 1], sh1, sc1)
        if l < n_a_layers:
            z = mm(x, [w_a_in], l, "gelu_bias", bn=wide, norm=mix_norm, bias=b_a_in[l:l + 1], out_dtype=F32)
            gated, v_rows = gate_fn(l, z)
            a_rows.append(v_rows)
            x = mm_res(gated, [w_a_out], l, bn=wide, res=x, gate=g1)
        else:
            j = l - n_a_layers
            q = mm(x, [w_q], j, "plain", bn=wide, norm=mix_norm, out_dtype=attn_fn.q_dtype)[0]
            x = mm_res(attn_fn(q, k_new, v_new), [w_o], j, bn=wide, res=x, gate=g1)
        h = _norm_call(x, g_ffn[l:l + 1], sh2, sc2, bm=min(bm, 512), rows_per_group=rows_per_group, out_dtype=BF16)
        a = mm(h, [w_ff_gate, w_ff_up], l, "swiglu", bn=narrow)
        x = mm_res(a, [w_ff_down], l, bn=narrow, res=x, gate=g2, k_steps=2 if d_ff % 512 == 0 else 1)
    f_shift, f_scale = fin_mod
    y = _norm_call(x, g_fin, f_shift, f_scale, bm=min(bm, 512), rows_per_group=rows_per_group, out_dtype=F32)
    return y, k_new, v_new, a_rows


def kernel(x_prompt, x_sample, c_prompt, c_sample, cache_k, cache_v, page_table, w_ada, b_ada, g_mix, g_ffn, w_a_in, b_a_in, ln_a_g, ln_a_b, w_s, b_s, w_a_out, w_kv_ada, b_kv_ada, g_kv, w_k, w_v, w_q, w_o, w_ff_gate, w_ff_up, w_ff_down, w_fin_ada, b_fin_ada, g_fin):
    batch, seq, d = x_prompt.shape
    n_seq, dec_seq, _ = x_sample.shape
    assert dec_seq == 1 and batch + n_seq <= C_ROWS
    depth = w_ada.shape[0]
    n_pages, page_size, n_heads, head_dim = cache_k.shape
    assert (page_size, n_heads, head_dim) == (PAGE_SIZE, N_HEADS, HEAD_DIM)
    pages_per_seq = page_table.shape[1]
    n_past_blocks = pages_per_seq // PAGES_PER_BLOCK

    c_rows = jnp.concatenate([c_prompt, c_sample, jnp.zeros((C_ROWS - batch - n_seq, d), F32)], axis=0)
    mods = _ada_call(c_rows, w_ada, b_ada[:, None, :])
    kv_mod = _ada_call(c_rows, w_kv_ada[None], b_kv_ada[None, None, :])[0]
    fin_mod = _ada_call(c_rows, w_fin_ada[None], b_fin_ada[None, None, :])[0]

    def prompt_vecs(rows, n):
        return [rows[:batch, i * d:(i + 1) * d][:, None, :] for i in range(n)]

    def sample_vecs(rows, n):
        return [rows[batch:batch + n_seq, i * d:(i + 1) * d][None] for i in range(n)]

    weights = (g_mix, g_ffn, w_a_in, b_a_in, w_a_out, g_kv[None], w_k[None], w_v[None], w_q, w_o,
               w_ff_gate, w_ff_up, w_ff_down, g_fin[None])
    b_s_t = jnp.swapaxes(b_s, 1, 2)

    def prompt_gate(l, z):
        return _gmlp_gate_call(z, ln_a_g[l:l + 1], ln_a_b[l:l + 1], w_s[l], b_s_t[l], rows=512), None

    def prompt_attn(q, k_new, v_new):
        o = _moba_call(q.reshape(batch, seq, d), k_new.reshape(batch, seq, d), v_new.reshape(batch, seq, d))
        return o.reshape(batch * seq, d)
    prompt_attn.q_dtype = BF16

    y_p, k_p, v_p, _ = _trunk(
        x_prompt.reshape(batch * seq, d),
        [prompt_vecs(mods[l], 6) for l in range(depth)], prompt_vecs(kv_mod, 2), prompt_vecs(fin_mod, 2),
        weights, bm=1024, rows_per_group=seq, gate_fn=prompt_gate, attn_fn=prompt_attn)

    pt_flat = page_table.reshape(-1)
    kmean = _kmean_call(pt_flat, cache_k, n_seq, n_past_blocks)
    group_w = d // A_GROUPS
    heads = (n_seq, N_HEADS, HEAD_DIM)

    def sample_gate(l, z):
        w00 = jnp.repeat(w_s[l, :, 0, 0], group_w)[None, :]
        b0 = jnp.repeat(b_s[l, :, 0], group_w)[None, :]
        return _gmlp_gate_row0_call(z, ln_a_g[l:l + 1], ln_a_b[l:l + 1], w00, b0)

    def sample_attn(q, k_new, v_new):
        q = q.reshape(heads)
        topk = _topk_call(q, kmean)[:, :, :MOBA_TOPK].reshape(-1)
        o = _decode_attn_call(pt_flat, topk, q, k_new.reshape(heads), v_new.reshape(heads),
                              cache_k, cache_v, pages_per_seq)
        return o.reshape(n_seq, d).astype(BF16)
    sample_attn.q_dtype = F32

    y_s, k_s, v_s, a_rows = _trunk(
        x_sample.reshape(n_seq, d),
        [sample_vecs(mods[l], 6) for l in range(depth)], sample_vecs(kv_mod, 2), sample_vecs(fin_mod, 2),
        weights, bm=n_seq, rows_per_group=n_seq, gate_fn=sample_gate, attn_fn=sample_attn)

    state_a_v = jnp.stack(a_rows, axis=0).reshape(len(a_rows), n_seq, 1, d)
    return (y_p.reshape(batch, seq, d), y_s.reshape(n_seq, 1, d),
            k_p.reshape(batch, seq, N_HEADS, HEAD_DIM), v_p.reshape(batch, seq, N_HEADS, HEAD_DIM),
            k_s.reshape(n_seq, 1, N_HEADS, HEAD_DIM), v_s.reshape(n_seq, 1, N_HEADS, HEAD_DIM),
            state_a_v)
```

```python
import functools

import jax
import jax.numpy as jnp
from jax import lax
from jax.experimental import pallas as pl
from jax.experimental.pallas import tpu as pltpu

F32 = jnp.float32
BF16 = jnp.bfloat16

N_HEADS = 16
HEAD_DIM = 128
CHUNK = 128
A_GROUPS = 16
MOBA_BLOCK = 256
MOBA_TOPK = 3
PAGE_SIZE = 128
PAGES_PER_BLOCK = MOBA_BLOCK // PAGE_SIZE
RMS_EPS = 1e-6
LN_EPS = 1e-5
NEG_INF = -1e30
ATTN_SCALE = HEAD_DIM ** -0.5
GATE_LANES = 128

C_ROWS = 16
MIB = 1024 * 1024
VMEM_LIMIT_V7X = 58 * MIB
COMPILER_TEMP_BYTES = 16 * MIB


def _params(n_axes, vmem_bytes):
    limit = min(int(vmem_bytes) + COMPILER_TEMP_BYTES, VMEM_LIMIT_V7X)
    return pltpu.CompilerParams(dimension_semantics=("arbitrary",) * n_axes,
                                vmem_limit_bytes=limit)


def _nbytes(shape, dtype):
    n = 1
    for s in shape:
        n *= s
    return n * jnp.dtype(dtype).itemsize


def _ada_kernel(c_ref, w_ref, b_ref, o_ref):
    sc = jax.nn.silu(c_ref[...]).astype(BF16)
    o_ref[...] = jnp.dot(sc, w_ref[...].astype(BF16), preferred_element_type=F32) + b_ref[...]


def _ada_call(c_rows, w, b):
    layers, k_dim, n_dim = w.shape
    tn = 1024
    vmem = 2 * (_nbytes((k_dim, tn), F32) + _nbytes((C_ROWS, k_dim), F32) + 2 * _nbytes((C_ROWS, tn), F32))
    return pl.pallas_call(
        _ada_kernel,
        grid=(layers, n_dim // tn),
        in_specs=[pl.BlockSpec((C_ROWS, k_dim), lambda l, n: (0, 0)),
                  pl.BlockSpec((None, k_dim, tn), lambda l, n: (l, 0, n)),
                  pl.BlockSpec((None, 1, tn), lambda l, n: (l, 0, n))],
        out_specs=pl.BlockSpec((None, C_ROWS, tn), lambda l, n: (l, 0, n)),
        out_shape=jax.ShapeDtypeStruct((layers, C_ROWS, n_dim), F32),
        compiler_params=_params(2, vmem),
        name="ada_modulation",
    )(c_rows, w, b)


def _norm_kernel(x_ref, g_ref, sh_ref, sc_ref, o_ref):
    x = x_ref[...]
    y = x * lax.rsqrt(jnp.mean(x * x, axis=-1, keepdims=True) + RMS_EPS) * g_ref[...]
    o_ref[...] = (y * (1.0 + sc_ref[...]) + sh_ref[...]).astype(o_ref.dtype)


def _norm_call(x, g, shift, scale, bm, rows_per_group, out_dtype):
    m_dim, d = x.shape
    r = shift.shape[1]
    blocks_per_group = rows_per_group // bm
    mod_spec = pl.BlockSpec((None, r, d), lambda m: (m // blocks_per_group, 0, 0))
    vmem = 2 * (_nbytes((bm, d), F32) + _nbytes((bm, d), out_dtype) + 3 * _nbytes((8, d), F32)) + 2 * _nbytes((bm, d), F32)
    return pl.pallas_call(
        _norm_kernel,
        grid=(m_dim // bm,),
        in_specs=[pl.BlockSpec((bm, d), lambda m: (m, 0)),
                  pl.BlockSpec((1, d), lambda m: (0, 0)),
                  mod_spec, mod_spec],
        out_specs=pl.BlockSpec((bm, d), lambda m: (m, 0)),
        out_shape=jax.ShapeDtypeStruct((m_dim, d), out_dtype),
        compiler_params=_params(1, vmem),
        name="rmsnorm_modulate",
    )(x, g, shift, scale)


def _mmc_kernel(*refs, n_w, epi, k_steps):
    x_ref = refs[0]
    w_refs = refs[1:1 + n_w]
    pos = 1 + n_w
    if epi == "residual":
        res_ref, gate_ref = refs[pos], refs[pos + 1]
        pos += 2
    o_ref = refs[pos]
    wb_refs = refs[pos + 1:pos + 1 + n_w]
    acc_refs = refs[pos + 1 + n_w:]
    m = pl.program_id(1)
    k = pl.program_id(2)

    @pl.when(m == 0)
    def _cast_weights():
        for w_ref, wb_ref in zip(w_refs, wb_refs):
            wb_ref[k] = w_ref[...].astype(BF16)

    x = x_ref[...]
    parts = [jnp.dot(x, wb_ref[k], preferred_element_type=F32) for wb_ref in wb_refs]

    def epilogue(accs):
        if epi == "swiglu":
            o_ref[...] = (jax.nn.silu(accs[0]) * accs[1]).astype(o_ref.dtype)
        elif epi == "residual":
            o_ref[...] = res_ref[...] + gate_ref[...] * accs[0]

    if k_steps == 1:
        epilogue(parts)
    else:
        @pl.when(k == 0)
        def _first():
            for a_ref, p in zip(acc_refs, parts):
                a_ref[...] = p

        if k_steps > 2:
            @pl.when((k > 0) & (k < k_steps - 1))
            def _accumulate():
                for a_ref, p in zip(acc_refs, parts):
                    a_ref[...] += p

        @pl.when(k == k_steps - 1)
        def _finish():
            epilogue([a_ref[...] + p for a_ref, p in zip(acc_refs, parts)])


def _mmc_call(x, ws, layer, epi, *, bm, bn, k_steps=1, out_dtype=BF16, res=None, gate=None, rows_per_group=None):
    m_dim, k_dim = x.shape
    n_dim = ws[0].shape[2]
    n_w = len(ws)
    bk = k_dim // k_steps
    grid = (n_dim // bn, m_dim // bm, k_steps)
    last_k = k_steps - 1

    def w_index(n, m, k):
        return (layer, jnp.where(m == 0, k, last_k), n)

    in_specs = [pl.BlockSpec((bm, bk), lambda n, m, k: (m, k))]
    in_specs += [pl.BlockSpec((None, bk, bn), w_index) for _ in ws]
    args = [x] + list(ws)
    vmem = 2 * _nbytes((bm, bk), x.dtype) + n_w * 2 * _nbytes((bk, bn), F32)
    if epi == "residual":
        r = gate.shape[1]
        blocks_per_group = rows_per_group // bm
        in_specs.append(pl.BlockSpec((bm, bn), lambda n, m, k: (m, n)))
        in_specs.append(pl.BlockSpec((None, r, bn), lambda n, m, k: (m // blocks_per_group, 0, n)))
        args += [res, gate]
        vmem += 2 * _nbytes((bm, bn), F32) + 2 * _nbytes((max(r, 8), bn), F32)
    vmem += 2 * _nbytes((bm, bn), out_dtype)
    scratch = [pltpu.VMEM((k_steps, bk, bn), BF16) for _ in ws]
    vmem += n_w * _nbytes((k_steps, bk, bn), BF16)
    if k_steps > 1:
        scratch += [pltpu.VMEM((bm, bn), F32) for _ in ws]
        vmem += n_w * _nbytes((bm, bn), F32)
    return pl.pallas_call(
        functools.partial(_mmc_kernel, n_w=n_w, epi=epi, k_steps=k_steps),
        grid=grid,
        in_specs=in_specs,
        out_specs=pl.BlockSpec((bm, bn), lambda n, m, k: (m, n)),
        out_shape=jax.ShapeDtypeStruct((m_dim, n_dim), out_dtype),
        scratch_shapes=scratch,
        compiler_params=_params(3, vmem),
        name="mmc_" + epi,
    )(*args)


def _mmr_kernel(*refs, epi, norm, n_steps):
    x_ref = refs[0]
    pos = 1
    if norm:
        g_ref, sh_ref, sc_ref = refs[1:4]
        pos = 4
    w_ref = refs[pos]
    pos += 1
    if epi in ("gelu", "gelu_ln"):
        bias_ref = refs[pos]
        pos += 1
    if epi == "gelu_ln":
        lng_ref, lnb_ref = refs[pos], refs[pos + 1]
        pos += 2
    if epi == "residual":
        res_ref, gate_ref = refs[pos], refs[pos + 1]
        pos += 2
    o_ref = refs[pos]
    wb_scr = refs[pos + 1]
    pos += 2
    if norm:
        h_scr = refs[pos]
        pos += 1
    if epi == "gelu_ln":
        z_scr = refs[pos]
    m = pl.program_id(0)
    n = pl.program_id(1)

    @pl.when(m == 0)
    def _cast_weights():
        wb_scr[n] = w_ref[...].astype(BF16)

    if norm:
        @pl.when(n == 0)
        def _normalize():
            x = x_ref[...]
            y = x * lax.rsqrt(jnp.mean(x * x, axis=-1, keepdims=True) + RMS_EPS) * g_ref[...]
            h_scr[...] = (y * (1.0 + sc_ref[...]) + sh_ref[...]).astype(BF16)
        x = h_scr[...]
    else:
        x = x_ref[...]
    acc = jnp.dot(x, wb_scr[n], preferred_element_type=F32)

    if epi == "plain":
        o_ref[...] = acc.astype(o_ref.dtype)
    elif epi == "gelu":
        o_ref[...] = jax.nn.gelu(acc + bias_ref[...]).astype(o_ref.dtype)
    elif epi == "residual":
        o_ref[...] = res_ref[...] + gate_ref[...] * acc
    elif epi == "gelu_ln":
        z_scr[n] = jax.nn.gelu(acc + bias_ref[...])

        @pl.when(n == n_steps - 1)
        def _layernorm():
            bn = z_scr.shape[2]
            inv_n = 1.0 / (n_steps * bn)
            zs = [z_scr[j] for j in range(n_steps)]
            mu = sum(jnp.sum(z, axis=-1, keepdims=True) for z in zs) * inv_n
            var = sum(jnp.sum(jnp.square(z - mu), axis=-1, keepdims=True) for z in zs) * inv_n
            rstd = lax.rsqrt(var + LN_EPS)
            for j, z in enumerate(zs):
                cols = slice(j * bn, (j + 1) * bn)
                o_ref[:, cols] = ((z - mu) * rstd * lng_ref[:, cols] + lnb_ref[:, cols]).astype(o_ref.dtype)


def _mmr_call(x, w, layer, epi, *, bm, bn, rows_per_group, out_dtype, n_cols=None, col_off=0, norm=None,
              bias=None, ln=None, res=None, gate=None):
    m_dim, k_dim = x.shape
    n_dim = w.shape[2] if n_cols is None else n_cols
    n_steps = n_dim // bn
    m_steps = m_dim // bm
    off = col_off // bn
    blocks_per_group = rows_per_group // bm

    def w_index(m, n):
        return (layer, 0, off + jnp.where(m == 0, n, n_steps - 1))

    w_buffers = 1 if m_steps > 1 else 2
    in_specs = [pl.BlockSpec((bm, k_dim), lambda m, n: (m, 0))]
    args = [x]
    vmem = 2 * _nbytes((bm, k_dim), x.dtype) + w_buffers * _nbytes((k_dim, bn), F32) + _nbytes((k_dim, n_dim), BF16)
    scratch = [pltpu.VMEM((n_steps, k_dim, bn), BF16)]
    if norm is not None:
        g, shift, scale = norm
        r = shift.shape[1]
        vec_spec = pl.BlockSpec((None, r, k_dim), lambda m, n: (m // blocks_per_group, 0, 0))
        in_specs += [pl.BlockSpec((1, k_dim), lambda m, n: (0, 0)), vec_spec, vec_spec]
        args += [g, shift, scale]
        scratch.append(pltpu.VMEM((bm, k_dim), BF16))
        vmem += 6 * _nbytes((max(r, 8), k_dim), F32) + _nbytes((bm, k_dim), BF16)
    in_specs.append(pl.BlockSpec((None, k_dim, bn), w_index, pipeline_mode=pl.Buffered(w_buffers)))
    args.append(w)
    if epi in ("gelu", "gelu_ln"):
        in_specs.append(pl.BlockSpec((1, bn), lambda m, n: (0, off + n)))
        args.append(bias)
        vmem += 2 * _nbytes((8, bn), F32)
    if epi == "gelu_ln":
        in_specs += [pl.BlockSpec((1, n_dim), lambda m, n: (0, 0))] * 2
        args += list(ln)
        scratch.append(pltpu.VMEM((n_steps, bm, bn), F32))
        vmem += 4 * _nbytes((8, n_dim), F32) + _nbytes((bm, n_dim), F32)
    if epi == "residual":
        r = gate.shape[1]
        in_specs.append(pl.BlockSpec((bm, bn), lambda m, n: (m, n)))
        in_specs.append(pl.BlockSpec((None, r, bn), lambda m, n: (m // blocks_per_group, 0, n)))
        args += [res, gate]
        vmem += 2 * _nbytes((bm, bn), F32) + 2 * _nbytes((max(r, 8), bn), F32)
    if epi == "gelu_ln":
        out_spec = pl.BlockSpec((bm, n_dim), lambda m, n: (m, 0))
        vmem += 2 * _nbytes((bm, n_dim), out_dtype)
    else:
        out_spec = pl.BlockSpec((bm, bn), lambda m, n: (m, n))
        vmem += 2 * _nbytes((bm, bn), out_dtype)
    return pl.pallas_call(
        functools.partial(_mmr_kernel, epi=epi, norm=norm is not None, n_steps=n_steps),
        grid=(m_steps, n_steps),
        in_specs=in_specs,
        out_specs=out_spec,
        out_shape=jax.ShapeDtypeStruct((m_dim, n_dim), out_dtype),
        scratch_shapes=scratch,
        compiler_params=_params(2, vmem),
        name="mmr_" + epi,
    )(*args)


def _gmlp_gate_kernel(u_ref, v_ref, ws_ref, bst_ref, o_ref, wm_scr, *, rows):
    group_w = u_ref.shape[1] // A_GROUPS

    @pl.when(pl.program_id(0) == 0)
    def _mask_weights():
        t = lax.broadcasted_iota(jnp.int32, (CHUNK, CHUNK), 0)
        s = lax.broadcasted_iota(jnp.int32, (CHUNK, CHUNK), 1)
        for g in range(A_GROUPS):
            wm_scr[g] = jnp.where(s <= t, ws_ref[g], 0.0).astype(BF16)

    for c in range(rows // CHUNK):
        chunk = slice(c * CHUNK, (c + 1) * CHUNK)
        for g in range(A_GROUPS):
            cols = slice(g * group_w, (g + 1) * group_w)
            mixed = jnp.dot(wm_scr[g], v_ref[chunk, cols], preferred_element_type=F32) + bst_ref[:, g:g + 1]
            o_ref[chunk, cols] = (u_ref[chunk, cols] * mixed).astype(o_ref.dtype)


def _gmlp_gate_call(u, v, w_s, b_s_t, rows):
    m_dim, d_a = u.shape
    row_spec = pl.BlockSpec((rows, d_a), lambda m: (m, 0))
    vmem = (6 * _nbytes((rows, d_a), BF16) + 3 * _nbytes((A_GROUPS, CHUNK, CHUNK), F32)
            + 2 * _nbytes((CHUNK, 128), F32))
    return pl.pallas_call(
        functools.partial(_gmlp_gate_kernel, rows=rows),
        grid=(m_dim // rows,),
        in_specs=[row_spec, row_spec,
                  pl.BlockSpec((A_GROUPS, CHUNK, CHUNK), lambda m: (0, 0, 0)),
                  pl.BlockSpec((CHUNK, A_GROUPS), lambda m: (0, 0))],
        out_specs=row_spec,
        out_shape=jax.ShapeDtypeStruct((m_dim, d_a), BF16),
        scratch_shapes=[pltpu.VMEM((A_GROUPS, CHUNK, CHUNK), BF16)],
        compiler_params=_params(1, vmem),
        name="gmlp_spatial_gate",
    )(u, v, w_s, b_s_t)


def _gmlp_gate_row0_kernel(u_ref, v_ref, w00_ref, b0_ref, o_ref):
    o_ref[...] = (u_ref[...] * (w00_ref[...] * v_ref[...] + b0_ref[...])).astype(o_ref.dtype)


def _gmlp_gate_row0_call(u, v, w00, b0):
    s_dim, d_a = u.shape
    full = lambda shape: pl.BlockSpec(shape, lambda i: (0,) * len(shape))
    vmem = 2 * (3 * _nbytes((max(s_dim, 16), d_a), F32) + 2 * _nbytes((8, d_a), F32))
    return pl.pallas_call(
        _gmlp_gate_row0_kernel,
        grid=(1,),
        in_specs=[full((s_dim, d_a)), full((s_dim, d_a)), full((1, d_a)), full((1, d_a))],
        out_specs=full((s_dim, d_a)),
        out_shape=jax.ShapeDtypeStruct((s_dim, d_a), BF16),
        compiler_params=_params(1, vmem),
        name="gmlp_gate_single_token",
    )(u, v, w00, b0)


GATE_ROWS = 16
LOG2_E = 1.4426950408889634


def _moba_kernel(q_ref, k_ref, v_ref, o_ref, kb_scr, vt_scr, km_scr, p_scr, *, n_blocks):
    contract_last = (((1,), (1,)), ((), ()))
    kb_scr[...] = k_ref[...].astype(BF16)
    vt_scr[...] = v_ref[...].T.astype(BF16)
    km_scr[...] = jnp.zeros_like(km_scr)
    for n in range(n_blocks):
        km_scr[n:n + 1, :] = jnp.mean(k_ref[n * MOBA_BLOCK:(n + 1) * MOBA_BLOCK, :], axis=0, keepdims=True)
    gate_all = lax.dot_general(km_scr[...].astype(BF16), q_ref[...], contract_last,
                               preferred_element_type=F32)
    blk = lax.broadcasted_iota(jnp.int32, (GATE_ROWS, MOBA_BLOCK), 0)
    key = lax.broadcasted_iota(jnp.int32, (MOBA_BLOCK, MOBA_BLOCK), 0)
    qry = lax.broadcasted_iota(jnp.int32, (MOBA_BLOCK, MOBA_BLOCK), 1)

    for i in range(n_blocks):
        rows = slice(i * MOBA_BLOCK, (i + 1) * MOBA_BLOCK)
        q_t = q_ref[rows, :]
        bias = [None] * i
        if i > MOBA_TOPK:
            gate = jnp.where(blk < i, gate_all[:, rows], NEG_INF)
            for n in range(i):
                g_n = gate[n:n + 1, :]
                beats = (gate > g_n) | ((gate == g_n) & (blk < n))
                cnt = jnp.sum(jnp.where(beats, 1.0, 0.0), axis=0, keepdims=True)
                bias[n] = jnp.where(cnt < MOBA_TOPK, 0.0, NEG_INF)
        kv_len = (i + 1) * MOBA_BLOCK
        s_all = lax.dot_general(kb_scr[0:kv_len, :], q_t, contract_last,
                                preferred_element_type=F32) * (ATTN_SCALE * LOG2_E)
        s_blocks = []
        for j in range(i + 1):
            s = s_all[j * MOBA_BLOCK:(j + 1) * MOBA_BLOCK, :]
            if j == i:
                s = jnp.where(key <= qry, s, NEG_INF)
            elif bias[j] is not None:
                s = s + bias[j]
            s_blocks.append(s)
        m = jnp.max(functools.reduce(jnp.maximum, s_blocks), axis=0, keepdims=True)
        l = None
        for j, s in enumerate(s_blocks):
            p = jnp.exp2(s - m)
            p_scr[j * MOBA_BLOCK:(j + 1) * MOBA_BLOCK, :] = p.astype(BF16)
            p_sum = jnp.sum(p, axis=0, keepdims=True)
            l = p_sum if l is None else l + p_sum
        acc = jnp.dot(vt_scr[:, 0:kv_len], p_scr[0:kv_len, :], preferred_element_type=F32)
        o_ref[rows, :] = (acc / l).T.astype(o_ref.dtype)


def _moba_call(q, k, v):
    b_dim, l_dim, hd = q.shape
    n_blocks = l_dim // MOBA_BLOCK
    assert n_blocks <= GATE_ROWS
    head_spec = pl.BlockSpec((None, l_dim, HEAD_DIM), lambda b, h: (b, 0, h))
    vmem = (4 * _nbytes((l_dim, HEAD_DIM), F32) + 6 * _nbytes((l_dim, HEAD_DIM), BF16)
            + _nbytes((GATE_ROWS, HEAD_DIM), F32) + _nbytes((l_dim, MOBA_BLOCK), BF16)
            + 2 * _nbytes((l_dim, MOBA_BLOCK), F32))
    return pl.pallas_call(
        functools.partial(_moba_kernel, n_blocks=n_blocks),
        grid=(b_dim, N_HEADS),
        in_specs=[head_spec, head_spec, head_spec],
        out_specs=head_spec,
        out_shape=jax.ShapeDtypeStruct((b_dim, l_dim, hd), BF16),
        scratch_shapes=[pltpu.VMEM((l_dim, HEAD_DIM), BF16), pltpu.VMEM((HEAD_DIM, l_dim), BF16),
                        pltpu.VMEM((GATE_ROWS, HEAD_DIM), F32), pltpu.VMEM((l_dim, MOBA_BLOCK), BF16)],
        compiler_params=_params(2, vmem),
        name="moba_prefill_attention",
    )(q, k, v)


KMEAN_BLOCKS_PER_STEP = 4


def _kmean_kernel(pt_ref, *refs):
    del pt_ref
    page_refs, o_ref = refs[:-1], refs[-1]
    first = pl.program_id(1) * KMEAN_BLOCKS_PER_STEP
    for t in range(KMEAN_BLOCKS_PER_STEP):
        pages = page_refs[t * PAGES_PER_BLOCK:(t + 1) * PAGES_PER_BLOCK]
        total = functools.reduce(jnp.add, [jnp.sum(p[...], axis=0) for p in pages])
        o_ref[first + t] = total * (1.0 / MOBA_BLOCK)


def _kmean_call(page_table_flat, cache_k, n_seq, n_blocks):
    pages_per_seq = n_blocks * PAGES_PER_BLOCK
    pages_per_step = KMEAN_BLOCKS_PER_STEP * PAGES_PER_BLOCK
    page_shape = (PAGE_SIZE, N_HEADS, HEAD_DIM)

    def page_spec(slot):
        return pl.BlockSpec((None,) + page_shape,
                            lambda b, n, pt: (pt[b * pages_per_seq + pages_per_step * n + slot], 0, 0, 0))

    vmem = 2 * pages_per_step * _nbytes(page_shape, F32) + 2 * _nbytes((n_blocks, N_HEADS, HEAD_DIM), F32)
    return pl.pallas_call(
        _kmean_kernel,
        grid_spec=pltpu.PrefetchScalarGridSpec(
            num_scalar_prefetch=1,
            grid=(n_seq, n_blocks // KMEAN_BLOCKS_PER_STEP),
            in_specs=[page_spec(slot) for slot in range(pages_per_step)],
            out_specs=pl.BlockSpec((None, n_blocks, N_HEADS, HEAD_DIM), lambda b, n, pt: (b, 0, 0, 0)),
        ),
        out_shape=jax.ShapeDtypeStruct((n_seq, n_blocks, N_HEADS, HEAD_DIM), F32),
        compiler_params=_params(2, vmem),
        name="paged_k_block_means",
    )(page_table_flat, *([cache_k] * pages_per_step))


def _topk_kernel(q_ref, km_ref, o_ref):
    n_blocks = km_ref.shape[0]
    km = km_ref[...].reshape(n_blocks * N_HEADS, HEAD_DIM).astype(BF16)
    gate = lax.dot_general(q_ref[...].astype(BF16), km, (((1,), (1,)), ((), ())), preferred_element_type=F32)
    lane = lax.broadcasted_iota(jnp.int32, gate.shape, 1)
    head = lax.broadcasted_iota(jnp.int32, gate.shape, 0)
    gate = jnp.where((lane & (N_HEADS - 1)) == head, gate, -jnp.inf)
    lane_f = lane.astype(F32)
    out_lane = lax.broadcasted_iota(jnp.int32, o_ref.shape, 1)
    out = jnp.zeros(o_ref.shape, jnp.int32)
    for t in range(MOBA_TOPK):
        best = jnp.max(gate, axis=-1, keepdims=True)
        arg = jnp.min(jnp.where(gate == best, lane_f, float(gate.shape[1])), axis=-1, keepdims=True)
        out = jnp.where(out_lane == t, arg.astype(jnp.int32) // N_HEADS, out)
        gate = jnp.where(lane_f == arg, -jnp.inf, gate)
    o_ref[...] = out


def _topk_call(q, kmean):
    s_dim = q.shape[0]
    n_blocks = kmean.shape[1]
    assert N_HEADS & (N_HEADS - 1) == 0
    vmem = 2 * (_nbytes((N_HEADS, HEAD_DIM), F32) + _nbytes((n_blocks, N_HEADS, HEAD_DIM), F32)
                + _nbytes((N_HEADS, GATE_LANES), jnp.int32)) + 8 * _nbytes((N_HEADS, n_blocks * N_HEADS), F32)
    return pl.pallas_call(
        _topk_kernel,
        grid=(s_dim,),
        in_specs=[pl.BlockSpec((None, N_HEADS, HEAD_DIM), lambda b: (b, 0, 0)),
                  pl.BlockSpec((None, n_blocks, N_HEADS, HEAD_DIM), lambda b: (b, 0, 0, 0))],
        out_specs=pl.BlockSpec((None, N_HEADS, GATE_LANES), lambda b: (b, 0, 0)),
        out_shape=jax.ShapeDtypeStruct((s_dim, N_HEADS, GATE_LANES), jnp.int32),
        compiler_params=_params(1, vmem),
        name="moba_decode_topk",
    )(q, kmean)


HEAD_TILE = 8
N_PICKED_PAGES = MOBA_TOPK * PAGES_PER_BLOCK


def _decode_attn_kernel(pt_ref, tk_ref, q_ref, kn_ref, vn_ref, *refs):
    del pt_ref, tk_ref
    k_refs = refs[:N_PICKED_PAGES]
    v_refs = refs[N_PICKED_PAGES:2 * N_PICKED_PAGES]
    o_ref = refs[2 * N_PICKED_PAGES]
    h = pl.program_id(1)
    q = q_ref[pl.ds(h, 1), :]
    s_own = jnp.sum(q * kn_ref[pl.ds(h, 1), :], axis=-1, keepdims=True) * ATTN_SCALE
    q_rows = jnp.broadcast_to(q, (HEAD_TILE, HEAD_DIM)).astype(BF16)
    rows = PAGE_SIZE * HEAD_TILE
    lane = lax.broadcasted_iota(jnp.int32, (HEAD_TILE, rows), 1)
    mine = (lane & (HEAD_TILE - 1)) == (h & (HEAD_TILE - 1))
    scores = []
    for k_ref in k_refs:
        k2 = k_ref[...].reshape(rows, HEAD_DIM).astype(BF16)
        s = lax.dot_general(q_rows, k2, (((1,), (1,)), ((), ())), preferred_element_type=F32) * ATTN_SCALE
        scores.append(jnp.where(mine, s, NEG_INF))
    m = s_own
    for s in scores:
        m = jnp.maximum(m, jnp.max(s, axis=-1, keepdims=True))
    l = jnp.exp(s_own - m)
    acc = l * vn_ref[pl.ds(h, 1), :]
    for s, v_ref in zip(scores, v_refs):
        p = jnp.exp(s - m)
        l = l + jnp.sum(p, axis=-1, keepdims=True)
        acc = acc + jnp.dot(p.astype(BF16), v_ref[...].reshape(rows, HEAD_DIM).astype(BF16), preferred_element_type=F32)
    o_ref[pl.ds(h, 1), :] = (acc / l)[0:1, :]


def _decode_attn_call(page_table_flat, topk_flat, q, k_new, v_new, cache_k, cache_v, pages_per_seq):
    s_dim = q.shape[0]
    row_spec = pl.BlockSpec((None, N_HEADS, HEAD_DIM), lambda b, h, pt, tk: (b, 0, 0))

    def page_spec(j, half):
        def index(b, h, pt, tk):
            block = tk[(b * N_HEADS + h) * MOBA_TOPK + j]
            return (pt[b * pages_per_seq + PAGES_PER_BLOCK * block + half], 0, h // HEAD_TILE, 0)
        return pl.BlockSpec((None, PAGE_SIZE, HEAD_TILE, HEAD_DIM), index)

    page_specs = [page_spec(j, half) for j in range(MOBA_TOPK) for half in range(PAGES_PER_BLOCK)]
    vmem = (4 * N_PICKED_PAGES * _nbytes((PAGE_SIZE, HEAD_TILE, HEAD_DIM), F32) + 8 * _nbytes((N_HEADS, HEAD_DIM), F32)
            + 4 * N_PICKED_PAGES * _nbytes((HEAD_TILE, PAGE_SIZE * HEAD_TILE), F32))
    return pl.pallas_call(
        _decode_attn_kernel,
        grid_spec=pltpu.PrefetchScalarGridSpec(
            num_scalar_prefetch=2,
            grid=(s_dim, N_HEADS),
            in_specs=[row_spec, row_spec, row_spec] + page_specs + page_specs,
            out_specs=row_spec,
        ),
        out_shape=jax.ShapeDtypeStruct((s_dim, N_HEADS, HEAD_DIM), F32),
        compiler_params=_params(2, vmem),
        name="moba_decode_attention",
    )(page_table_flat, topk_flat, q, k_new, v_new, *([cache_k] * N_PICKED_PAGES), *([cache_v] * N_PICKED_PAGES))


def _trunk(x, mods, kv_mod, fin_mod, weights, *, bm, rows_per_group, gate_fn, attn_fn, q_dtype, v_dtype):
    (g_mix, g_ffn, w_a_in, b_a_in, ln_a_g, ln_a_b, w_a_out, g_kv, w_k, w_v, w_q, w_o,
     w_ff_gate, w_ff_up, w_ff_down, g_fin) = weights
    m_dim, d = x.shape
    depth = g_mix.shape[0]
    n_a_layers = w_a_in.shape[0]
    d_a = w_a_in.shape[2] // 2
    half_bm = max(bm // 2, 8)
    mmr = functools.partial(_mmr_call, bm=bm, bn=min(1024, d), rows_per_group=rows_per_group)
    ffn_bm = 2 * bm if m_dim % (2 * bm) == 0 else bm
    a_rows = []
    k_new = v_new = None
    for l in range(depth):
        if l == n_a_layers:
            kv_norm = (g_kv,) + tuple(kv_mod)
            k_new = mmr(x, w_k, 0, "plain", norm=kv_norm, out_dtype=F32)
            v_new = mmr(x, w_v, 0, "plain", norm=kv_norm, out_dtype=F32)
        sh1, sc1, g1, sh2, sc2, g2 = mods[l]
        mix_norm = (g_mix[l:l + 1], sh1, sc1)
        if l < n_a_layers:
            bias = b_a_in[l:l + 1]
            u = mmr(x, w_a_in, l, "gelu", norm=mix_norm, bias=bias, n_cols=d_a, out_dtype=BF16)
            v_ln = _mmr_call(x, w_a_in, l, "gelu_ln", bm=half_bm, bn=min(1024, d), rows_per_group=rows_per_group,
                             norm=mix_norm, bias=bias, ln=(ln_a_g[l:l + 1], ln_a_b[l:l + 1]),
                             n_cols=d_a, col_off=d_a, out_dtype=v_dtype)
            a_rows.append(v_ln)
            x = mmr(gate_fn(l, u, v_ln), w_a_out, l, "residual", res=x, gate=g1, out_dtype=F32)
        else:
            j = l - n_a_layers
            q = mmr(x, w_q, j, "plain", norm=mix_norm, out_dtype=q_dtype)
            x = mmr(attn_fn(q, k_new, v_new), w_o, j, "residual", res=x, gate=g1, out_dtype=F32)
        h = _norm_call(x, g_ffn[l:l + 1], sh2, sc2, bm=min(bm, 512), rows_per_group=rows_per_group, out_dtype=BF16)
        a = _mmc_call(h, [w_ff_gate, w_ff_up], l, "swiglu", bm=ffn_bm, bn=512)
        x = _mmc_call(a, [w_ff_down], l, "residual", bm=bm, bn=512, k_steps=2, out_dtype=F32, res=x, gate=g2,
                      rows_per_group=rows_per_group)
    f_shift, f_scale = fin_mod
    y = _norm_call(x, g_fin, f_shift, f_scale, bm=min(bm, 512), rows_per_group=rows_per_group, out_dtype=F32)
    return y, k_new, v_new, a_rows


def kernel(x_prompt, x_sample, c_prompt, c_sample, cache_k, cache_v, page_table, w_ada, b_ada, g_mix, g_ffn, w_a_in, b_a_in, ln_a_g, ln_a_b, w_s, b_s, w_a_out, w_kv_ada, b_kv_ada, g_kv, w_k, w_v, w_q, w_o, w_ff_gate, w_ff_up, w_ff_down, w_fin_ada, b_fin_ada, g_fin):
    batch, seq, d = x_prompt.shape
    n_seq, dec_seq, _ = x_sample.shape
    assert dec_seq == 1 and batch + n_seq <= C_ROWS
    depth = w_ada.shape[0]
    n_pages, page_size, n_heads, head_dim = cache_k.shape
    assert (page_size, n_heads, head_dim) == (PAGE_SIZE, N_HEADS, HEAD_DIM)
    pages_per_seq = page_table.shape[1]
    n_past_blocks = pages_per_seq // PAGES_PER_BLOCK

    c_rows = jnp.concatenate([c_prompt, c_sample, jnp.zeros((C_ROWS - batch - n_seq, d), F32)], axis=0)
    mods = _ada_call(c_rows, w_ada, b_ada[:, None, :])
    kv_mod = _ada_call(c_rows, w_kv_ada[None], b_kv_ada[None, None, :])[0]
    fin_mod = _ada_call(c_rows, w_fin_ada[None], b_fin_ada[None, None, :])[0]

    def prompt_vecs(rows, n):
        return [rows[:batch, i * d:(i + 1) * d][:, None, :] for i in range(n)]

    def sample_vecs(rows, n):
        return [rows[batch:batch + n_seq, i * d:(i + 1) * d][None] for i in range(n)]

    weights = (g_mix, g_ffn, w_a_in, b_a_in, ln_a_g, ln_a_b, w_a_out, g_kv[None], w_k[None], w_v[None], w_q, w_o,
               w_ff_gate, w_ff_up, w_ff_down, g_fin[None])
    b_s_t = jnp.swapaxes(b_s, 1, 2)

    def prompt_gate(l, u, v_ln):
        return _gmlp_gate_call(u, v_ln, w_s[l], b_s_t[l], rows=512)

    def prompt_attn(q, k_new, v_new):
        o = _moba_call(q.reshape(batch, seq, d), k_new.reshape(batch, seq, d), v_new.reshape(batch, seq, d))
        return o.reshape(batch * seq, d)

    y_p, k_p, v_p, _ = _trunk(
        x_prompt.reshape(batch * seq, d),
        [prompt_vecs(mods[l], 6) for l in range(depth)], prompt_vecs(kv_mod, 2), prompt_vecs(fin_mod, 2),
        weights, bm=min(1024, seq), rows_per_group=seq, gate_fn=prompt_gate, attn_fn=prompt_attn,
        q_dtype=BF16, v_dtype=BF16)

    pt_flat = page_table.reshape(-1)
    kmean = _kmean_call(pt_flat, cache_k, n_seq, n_past_blocks)
    group_w = d // A_GROUPS
    heads = (n_seq, N_HEADS, HEAD_DIM)

    def sample_gate(l, u, v_ln):
        w00 = jnp.repeat(w_s[l, :, 0, 0], group_w)[None, :]
        b0 = jnp.repeat(b_s[l, :, 0], group_w)[None, :]
        return _gmlp_gate_row0_call(u, v_ln, w00, b0)

    def sample_attn(q, k_new, v_new):
        q = q.reshape(heads)
        topk = _topk_call(q, kmean)[:, :, :MOBA_TOPK].reshape(-1)
        o = _decode_attn_call(pt_flat, topk, q, k_new.reshape(heads), v_new.reshape(heads),
                              cache_k, cache_v, pages_per_seq)
        return o.reshape(n_seq, d).astype(BF16)

    y_s, k_s, v_s, a_rows = _trunk(
        x_sample.reshape(n_seq, d),
        [sample_vecs(mods[l], 6) for l in range(depth)], sample_vecs(kv_mod, 2), sample_vecs(fin_mod, 2),
        weights, bm=n_seq, rows_per_group=n_seq, gate_fn=sample_gate, attn_fn=sample_attn,
        q_dtype=F32, v_dtype=F32)

    state_a_v = jnp.stack(a_rows, axis=0).reshape(len(a_rows), n_seq, 1, d)
    return (y_p.reshape(batch, seq, d), y_s.reshape(n_seq, 1, d),
            k_p.reshape(batch, seq, N_HEADS, HEAD_DIM), v_p.reshape(batch, seq, N_HEADS, HEAD_DIM),
            k_s.reshape(n_seq, 1, N_HEADS, HEAD_DIM), v_s.reshape(n_seq, 1, N_HEADS, HEAD_DIM),
            state_a_v)
```

```python
import functools

import jax
import jax.numpy as jnp
from jax import lax
from jax.experimental import pallas as pl
from jax.experimental.pallas import tpu as pltpu

F32 = jnp.float32
BF16 = jnp.bfloat16

N_HEADS = 16
HEAD_DIM = 128
CHUNK = 128
A_GROUPS = 16
MOBA_BLOCK = 256
MOBA_TOPK = 3
PAGE_SIZE = 128
PAGES_PER_BLOCK = MOBA_BLOCK // PAGE_SIZE
RMS_EPS = 1e-6
LN_EPS = 1e-5
NEG_INF = -1e30
ATTN_SCALE = HEAD_DIM ** -0.5
GATE_LANES = 128

C_ROWS = 16
MIB = 1024 * 1024
VMEM_LIMIT_V7X = 58 * MIB
COMPILER_TEMP_BYTES = 16 * MIB


def _params(n_axes, vmem_bytes):
    limit = min(int(vmem_bytes) + COMPILER_TEMP_BYTES, VMEM_LIMIT_V7X)
    return pltpu.CompilerParams(dimension_semantics=("arbitrary",) * n_axes,
                                vmem_limit_bytes=limit)


def _nbytes(shape, dtype):
    n = 1
    for s in shape:
        n *= s
    return n * jnp.dtype(dtype).itemsize


def _ada_kernel(c_ref, w_ref, b_ref, o_ref):
    sc = jax.nn.silu(c_ref[...]).astype(BF16)
    o_ref[...] = jnp.dot(sc, w_ref[...].astype(BF16), preferred_element_type=F32) + b_ref[...]


def _ada_call(c_rows, w, b):
    layers, k_dim, n_dim = w.shape
    tn = 1024
    vmem = 2 * (_nbytes((k_dim, tn), F32) + _nbytes((C_ROWS, k_dim), F32) + 2 * _nbytes((C_ROWS, tn), F32))
    return pl.pallas_call(
        _ada_kernel,
        grid=(layers, n_dim // tn),
        in_specs=[pl.BlockSpec((C_ROWS, k_dim), lambda l, n: (0, 0)),
                  pl.BlockSpec((None, k_dim, tn), lambda l, n: (l, 0, n)),
                  pl.BlockSpec((None, 1, tn), lambda l, n: (l, 0, n))],
        out_specs=pl.BlockSpec((None, C_ROWS, tn), lambda l, n: (l, 0, n)),
        out_shape=jax.ShapeDtypeStruct((layers, C_ROWS, n_dim), F32),
        compiler_params=_params(2, vmem),
        name="ada_modulation",
    )(c_rows, w, b)


def _norm_kernel(x_ref, g_ref, sh_ref, sc_ref, o_ref):
    x = x_ref[...]
    y = x * lax.rsqrt(jnp.mean(x * x, axis=-1, keepdims=True) + RMS_EPS) * g_ref[...]
    o_ref[...] = (y * (1.0 + sc_ref[...]) + sh_ref[...]).astype(o_ref.dtype)


def _norm_call(x, g, shift, scale, bm, rows_per_group, out_dtype):
    m_dim, d = x.shape
    r = shift.shape[1]
    blocks_per_group = rows_per_group // bm
    mod_spec = pl.BlockSpec((None, r, d), lambda m: (m // blocks_per_group, 0, 0))
    vmem = 2 * (_nbytes((bm, d), F32) + _nbytes((bm, d), out_dtype) + 3 * _nbytes((8, d), F32)) + 2 * _nbytes((bm, d), F32)
    return pl.pallas_call(
        _norm_kernel,
        grid=(m_dim // bm,),
        in_specs=[pl.BlockSpec((bm, d), lambda m: (m, 0)),
                  pl.BlockSpec((1, d), lambda m: (0, 0)),
                  mod_spec, mod_spec],
        out_specs=pl.BlockSpec((bm, d), lambda m: (m, 0)),
        out_shape=jax.ShapeDtypeStruct((m_dim, d), out_dtype),
        compiler_params=_params(1, vmem),
        name="rmsnorm_modulate",
    )(x, g, shift, scale)


def _mmc_kernel(*refs, n_w, epi, k_steps, km_blocks):
    if km_blocks:
        refs = refs[1:]
    x_ref = refs[0]
    w_refs = refs[1:1 + n_w]
    pos = 1 + n_w
    if epi == "residual":
        res_ref, gate_ref = refs[pos], refs[pos + 1]
        pos += 2
    page_refs = refs[pos:pos + km_blocks * PAGES_PER_BLOCK]
    pos += km_blocks * PAGES_PER_BLOCK
    o_ref = refs[pos]
    pos += 1
    if km_blocks:
        km_ref = refs[pos]
        pos += 1
        for t in range(km_blocks):
            pages = page_refs[t * PAGES_PER_BLOCK:(t + 1) * PAGES_PER_BLOCK]
            total = functools.reduce(jnp.add, [jnp.sum(p[...], axis=0) for p in pages])
            km_ref[t] = total * (1.0 / MOBA_BLOCK)
    wb_refs = refs[pos:pos + n_w]
    acc_refs = refs[pos + n_w:]
    m = pl.program_id(1)
    k = pl.program_id(2)

    @pl.when(m == 0)
    def _cast_weights():
        for w_ref, wb_ref in zip(w_refs, wb_refs):
            wb_ref[k] = w_ref[...].astype(BF16)

    x = x_ref[...]
    parts = [jnp.dot(x, wb_ref[k], preferred_element_type=F32) for wb_ref in wb_refs]

    def epilogue(accs):
        if epi == "swiglu":
            o_ref[...] = (jax.nn.silu(accs[0]) * accs[1]).astype(o_ref.dtype)
        elif epi == "residual":
            o_ref[...] = res_ref[...] + gate_ref[...] * accs[0]

    if k_steps == 1:
        epilogue(parts)
    else:
        @pl.when(k == 0)
        def _first():
            for a_ref, p in zip(acc_refs, parts):
                a_ref[...] = p

        if k_steps > 2:
            @pl.when((k > 0) & (k < k_steps - 1))
            def _accumulate():
                for a_ref, p in zip(acc_refs, parts):
                    a_ref[...] += p

        @pl.when(k == k_steps - 1)
        def _finish():
            epilogue([a_ref[...] + p for a_ref, p in zip(acc_refs, parts)])


def _mmc_steps(m_dim, n_dim, bm, bn):
    return (n_dim // bn) * (m_dim // bm)


def _mmc_call(x, ws, layer, epi, *, bm, bn, k_steps=1, out_dtype=BF16, res=None, gate=None, rows_per_group=None,
              kmean=None):
    m_dim, k_dim = x.shape
    n_dim = ws[0].shape[2]
    n_w = len(ws)
    bk = k_dim // k_steps
    m_steps = m_dim // bm
    grid = (n_dim // bn, m_steps, k_steps)
    last_k = k_steps - 1

    def w_index(n, m, k, *_):
        return (layer, jnp.where(m == 0, k, last_k), n)

    in_specs = [pl.BlockSpec((bm, bk), lambda n, m, k, *_: (m, k))]
    in_specs += [pl.BlockSpec((None, bk, bn), w_index) for _ in ws]
    args = [x] + list(ws)
    vmem = 2 * _nbytes((bm, bk), x.dtype) + n_w * 2 * _nbytes((bk, bn), F32)
    if epi == "residual":
        r = gate.shape[1]
        blocks_per_group = rows_per_group // bm
        in_specs.append(pl.BlockSpec((bm, bn), lambda n, m, k, *_: (m, n)))
        in_specs.append(pl.BlockSpec((None, r, bn), lambda n, m, k, *_: (m // blocks_per_group, 0, n)))
        args += [res, gate]
        vmem += 2 * _nbytes((bm, bn), F32) + 2 * _nbytes((max(r, 8), bn), F32)
    out_specs = [pl.BlockSpec((bm, bn), lambda n, m, k, *_: (m, n))]
    out_shape = [jax.ShapeDtypeStruct((m_dim, n_dim), out_dtype)]
    vmem += 2 * _nbytes((bm, bn), out_dtype)
    km_blocks = 0
    if kmean is not None:
        assert k_steps == 1
        page_table_flat, cache_k, first_block, km_blocks = kmean
        last_block = (page_table_flat.shape[0] // PAGES_PER_BLOCK) - 1
        page_shape = (PAGE_SIZE, N_HEADS, HEAD_DIM)

        def page_spec(t, half):
            def index(n, m, k, pt):
                g = jnp.minimum(first_block + (n * m_steps + m) * km_blocks + t, last_block)
                return (pt[g * PAGES_PER_BLOCK + half], 0, 0, 0)
            return pl.BlockSpec((None,) + page_shape, index)

        in_specs += [page_spec(t, half) for t in range(km_blocks) for half in range(PAGES_PER_BLOCK)]
        args += [cache_k] * (km_blocks * PAGES_PER_BLOCK)
        out_specs.append(pl.BlockSpec((km_blocks, N_HEADS, HEAD_DIM), lambda n, m, k, pt: (n * m_steps + m, 0, 0)))
        out_shape.append(jax.ShapeDtypeStruct((grid[0] * m_steps * km_blocks, N_HEADS, HEAD_DIM), F32))
        vmem += 2 * km_blocks * PAGES_PER_BLOCK * _nbytes(page_shape, F32) + 2 * km_blocks * _nbytes((N_HEADS, HEAD_DIM), F32)
        args = [page_table_flat] + args
    scratch = [pltpu.VMEM((k_steps, bk, bn), BF16) for _ in ws]
    vmem += n_w * _nbytes((k_steps, bk, bn), BF16)
    if k_steps > 1:
        scratch += [pltpu.VMEM((bm, bn), F32) for _ in ws]
        vmem += n_w * _nbytes((bm, bn), F32)
    outs = pl.pallas_call(
        functools.partial(_mmc_kernel, n_w=n_w, epi=epi, k_steps=k_steps, km_blocks=km_blocks),
        grid_spec=pltpu.PrefetchScalarGridSpec(
            num_scalar_prefetch=1 if kmean is not None else 0,
            grid=grid,
            in_specs=in_specs,
            out_specs=out_specs,
            scratch_shapes=scratch,
        ),
        out_shape=out_shape,
        compiler_params=_params(3, vmem),
        name="mmc_" + epi,
    )(*args)
    return outs[0] if kmean is None else tuple(outs)


def _mmr_kernel(*refs, epi, norm, n_steps):
    x_ref = refs[0]
    pos = 1
    if norm:
        g_ref, sh_ref, sc_ref = refs[1:4]
        pos = 4
    w_ref = refs[pos]
    pos += 1
    if epi in ("gelu", "gelu_ln"):
        bias_ref = refs[pos]
        pos += 1
    if epi == "gelu_ln":
        lng_ref, lnb_ref = refs[pos], refs[pos + 1]
        pos += 2
    if epi == "residual":
        res_ref, gate_ref = refs[pos], refs[pos + 1]
        pos += 2
    o_ref = refs[pos]
    wb_scr = refs[pos + 1]
    pos += 2
    if norm:
        h_scr = refs[pos]
        pos += 1
    if epi == "gelu_ln":
        z_scr = refs[pos]
    m = pl.program_id(0)
    n = pl.program_id(1)

    @pl.when(m == 0)
    def _cast_weights():
        wb_scr[n] = w_ref[...].astype(BF16)

    if norm:
        @pl.when(n == 0)
        def _normalize():
            x = x_ref[...]
            y = x * lax.rsqrt(jnp.mean(x * x, axis=-1, keepdims=True) + RMS_EPS) * g_ref[...]
            h_scr[...] = (y * (1.0 + sc_ref[...]) + sh_ref[...]).astype(BF16)
        x = h_scr[...]
    else:
        x = x_ref[...]
    acc = jnp.dot(x, wb_scr[n], preferred_element_type=F32)

    if epi == "plain":
        o_ref[...] = acc.astype(o_ref.dtype)
    elif epi == "gelu":
        o_ref[...] = jax.nn.gelu(acc + bias_ref[...]).astype(o_ref.dtype)
    elif epi == "residual":
        o_ref[...] = res_ref[...] + gate_ref[...] * acc
    elif epi == "gelu_ln":
        z_scr[n] = jax.nn.gelu(acc + bias_ref[...])

        @pl.when(n == n_steps - 1)
        def _layernorm():
            bn = z_scr.shape[2]
            inv_n = 1.0 / (n_steps * bn)
            zs = [z_scr[j] for j in range(n_steps)]
            mu = sum(jnp.sum(z, axis=-1, keepdims=True) for z in zs) * inv_n
            var = sum(jnp.sum(jnp.square(z - mu), axis=-1, keepdims=True) for z in zs) * inv_n
            rstd = lax.rsqrt(var + LN_EPS)
            for j, z in enumerate(zs):
                cols = slice(j * bn, (j + 1) * bn)
                o_ref[:, cols] = ((z - mu) * rstd * lng_ref[:, cols] + lnb_ref[:, cols]).astype(o_ref.dtype)


def _mmr_call(x, w, layer, epi, *, bm, bn, rows_per_group, out_dtype, n_cols=None, col_off=0, norm=None,
              bias=None, ln=None, res=None, gate=None):
    m_dim, k_dim = x.shape
    n_dim = w.shape[2] if n_cols is None else n_cols
    n_steps = n_dim // bn
    m_steps = m_dim // bm
    off = col_off // bn
    blocks_per_group = rows_per_group // bm

    def w_index(m, n):
        return (layer, 0, off + jnp.where(m == 0, n, n_steps - 1))

    w_buffers = 1 if m_steps > 1 else 2
    in_specs = [pl.BlockSpec((bm, k_dim), lambda m, n: (m, 0))]
    args = [x]
    vmem = 2 * _nbytes((bm, k_dim), x.dtype) + w_buffers * _nbytes((k_dim, bn), F32) + _nbytes((k_dim, n_dim), BF16)
    scratch = [pltpu.VMEM((n_steps, k_dim, bn), BF16)]
    if norm is not None:
        g, shift, scale = norm
        r = shift.shape[1]
        vec_spec = pl.BlockSpec((None, r, k_dim), lambda m, n: (m // blocks_per_group, 0, 0))
        in_specs += [pl.BlockSpec((1, k_dim), lambda m, n: (0, 0)), vec_spec, vec_spec]
        args += [g, shift, scale]
        scratch.append(pltpu.VMEM((bm, k_dim), BF16))
        vmem += 6 * _nbytes((max(r, 8), k_dim), F32) + _nbytes((bm, k_dim), BF16)
    in_specs.append(pl.BlockSpec((None, k_dim, bn), w_index, pipeline_mode=pl.Buffered(w_buffers)))
    args.append(w)
    if epi in ("gelu", "gelu_ln"):
        in_specs.append(pl.BlockSpec((1, bn), lambda m, n: (0, off + n)))
        args.append(bias)
        vmem += 2 * _nbytes((8, bn), F32)
    if epi == "gelu_ln":
        in_specs += [pl.BlockSpec((1, n_dim), lambda m, n: (0, 0))] * 2
        args += list(ln)
        scratch.append(pltpu.VMEM((n_steps, bm, bn), F32))
        vmem += 4 * _nbytes((8, n_dim), F32) + _nbytes((bm, n_dim), F32)
    if epi == "residual":
        r = gate.shape[1]
        in_specs.append(pl.BlockSpec((bm, bn), lambda m, n: (m, n)))
        in_specs.append(pl.BlockSpec((None, r, bn), lambda m, n: (m // blocks_per_group, 0, n)))
        args += [res, gate]
        vmem += 2 * _nbytes((bm, bn), F32) + 2 * _nbytes((max(r, 8), bn), F32)
    if epi == "gelu_ln":
        out_spec = pl.BlockSpec((bm, n_dim), lambda m, n: (m, 0))
        vmem += 2 * _nbytes((bm, n_dim), out_dtype)
    else:
        out_spec = pl.BlockSpec((bm, bn), lambda m, n: (m, n))
        vmem += 2 * _nbytes((bm, bn), out_dtype)
    return pl.pallas_call(
        functools.partial(_mmr_kernel, epi=epi, norm=norm is not None, n_steps=n_steps),
        grid=(m_steps, n_steps),
        in_specs=in_specs,
        out_specs=out_spec,
        out_shape=jax.ShapeDtypeStruct((m_dim, n_dim), out_dtype),
        scratch_shapes=scratch,
        compiler_params=_params(2, vmem),
        name="mmr_" + epi,
    )(*args)


def _gmlp_gate_kernel(u_ref, v_ref, ws_ref, bst_ref, o_ref, wm_scr, *, rows):
    group_w = u_ref.shape[1] // A_GROUPS

    @pl.when(pl.program_id(0) == 0)
    def _mask_weights():
        t = lax.broadcasted_iota(jnp.int32, (CHUNK, CHUNK), 0)
        s = lax.broadcasted_iota(jnp.int32, (CHUNK, CHUNK), 1)
        for g in range(A_GROUPS):
            wm_scr[g] = jnp.where(s <= t, ws_ref[g], 0.0).astype(BF16)

    for c in range(rows // CHUNK):
        chunk = slice(c * CHUNK, (c + 1) * CHUNK)
        for g in range(A_GROUPS):
            cols = slice(g * group_w, (g + 1) * group_w)
            mixed = jnp.dot(wm_scr[g], v_ref[chunk, cols], preferred_element_type=F32) + bst_ref[:, g:g + 1]
            o_ref[chunk, cols] = (u_ref[chunk, cols] * mixed).astype(o_ref.dtype)


def _gmlp_gate_call(u, v, w_s, b_s_t, rows):
    m_dim, d_a = u.shape
    row_spec = pl.BlockSpec((rows, d_a), lambda m: (m, 0))
    vmem = (6 * _nbytes((rows, d_a), BF16) + 3 * _nbytes((A_GROUPS, CHUNK, CHUNK), F32)
            + 2 * _nbytes((CHUNK, 128), F32))
    return pl.pallas_call(
        functools.partial(_gmlp_gate_kernel, rows=rows),
        grid=(m_dim // rows,),
        in_specs=[row_spec, row_spec,
                  pl.BlockSpec((A_GROUPS, CHUNK, CHUNK), lambda m: (0, 0, 0)),
                  pl.BlockSpec((CHUNK, A_GROUPS), lambda m: (0, 0))],
        out_specs=row_spec,
        out_shape=jax.ShapeDtypeStruct((m_dim, d_a), BF16),
        scratch_shapes=[pltpu.VMEM((A_GROUPS, CHUNK, CHUNK), BF16)],
        compiler_params=_params(1, vmem),
        name="gmlp_spatial_gate",
    )(u, v, w_s, b_s_t)


def _gmlp_gate_row0_kernel(u_ref, v_ref, w00_ref, b0_ref, o_ref):
    o_ref[...] = (u_ref[...] * (w00_ref[...] * v_ref[...] + b0_ref[...])).astype(o_ref.dtype)


def _gmlp_gate_row0_call(u, v, w00, b0):
    s_dim, d_a = u.shape
    full = lambda shape: pl.BlockSpec(shape, lambda i: (0,) * len(shape))
    vmem = 2 * (3 * _nbytes((max(s_dim, 16), d_a), F32) + 2 * _nbytes((8, d_a), F32))
    return pl.pallas_call(
        _gmlp_gate_row0_kernel,
        grid=(1,),
        in_specs=[full((s_dim, d_a)), full((s_dim, d_a)), full((1, d_a)), full((1, d_a))],
        out_specs=full((s_dim, d_a)),
        out_shape=jax.ShapeDtypeStruct((s_dim, d_a), BF16),
        compiler_params=_params(1, vmem),
        name="gmlp_gate_single_token",
    )(u, v, w00, b0)


GATE_ROWS = 16
LOG2_E = 1.4426950408889634


HEAD_TILE = 8
N_PICKED_PAGES = MOBA_TOPK * PAGES_PER_BLOCK


def _moba_kernel(pt_ref, tk_ref, q_ref, k_ref, v_ref, dq_ref, dkn_ref, dvn_ref, *refs, n_blocks, pairs):
    del pt_ref, tk_ref
    n_pages = pairs * N_PICKED_PAGES
    dk_refs, dv_refs = refs[:n_pages], refs[n_pages:2 * n_pages]
    o_ref, do_ref, kb_scr, vt_scr, km_scr, p_scr = refs[2 * n_pages:]
    _moba_prefill(q_ref, k_ref, v_ref, o_ref, kb_scr, vt_scr, km_scr, p_scr, n_blocks)
    first = (pl.program_id(0) * N_HEADS + pl.program_id(1)) * pairs
    for t in range(pairs):
        pair = first + t
        seq, head = pair // N_HEADS, pair % N_HEADS
        picked = slice(t * N_PICKED_PAGES, (t + 1) * N_PICKED_PAGES)
        do_ref[seq, pl.ds(head, 1), :] = _decode_attend(
            dq_ref[seq, pl.ds(head, 1), :], dkn_ref[seq, pl.ds(head, 1), :], dvn_ref[seq, pl.ds(head, 1), :],
            dk_refs[picked], dv_refs[picked], head)


def _moba_prefill(q_ref, k_ref, v_ref, o_ref, kb_scr, vt_scr, km_scr, p_scr, n_blocks):
    contract_last = (((1,), (1,)), ((), ()))
    kb_scr[...] = k_ref[...].astype(BF16)
    vt_scr[...] = v_ref[...].T.astype(BF16)
    km_scr[...] = jnp.zeros_like(km_scr)
    for n in range(n_blocks):
        km_scr[n:n + 1, :] = jnp.mean(k_ref[n * MOBA_BLOCK:(n + 1) * MOBA_BLOCK, :], axis=0, keepdims=True)
    gate_all = lax.dot_general(km_scr[...].astype(BF16), q_ref[...], contract_last,
                               preferred_element_type=F32)
    blk = lax.broadcasted_iota(jnp.int32, (GATE_ROWS, MOBA_BLOCK), 0)
    key = lax.broadcasted_iota(jnp.int32, (MOBA_BLOCK, MOBA_BLOCK), 0)
    qry = lax.broadcasted_iota(jnp.int32, (MOBA_BLOCK, MOBA_BLOCK), 1)

    for i in range(n_blocks):
        rows = slice(i * MOBA_BLOCK, (i + 1) * MOBA_BLOCK)
        q_t = q_ref[rows, :]
        bias = [None] * i
        if i > MOBA_TOPK:
            gate = jnp.where(blk < i, gate_all[:, rows], NEG_INF)
            for n in range(i):
                g_n = gate[n:n + 1, :]
                beats = (gate > g_n) | ((gate == g_n) & (blk < n))
                cnt = jnp.sum(jnp.where(beats, 1.0, 0.0), axis=0, keepdims=True)
                bias[n] = jnp.where(cnt < MOBA_TOPK, 0.0, NEG_INF)
        kv_len = (i + 1) * MOBA_BLOCK
        s_all = lax.dot_general(kb_scr[0:kv_len, :], q_t, contract_last,
                                preferred_element_type=F32) * (ATTN_SCALE * LOG2_E)
        s_blocks = []
        for j in range(i + 1):
            s = s_all[j * MOBA_BLOCK:(j + 1) * MOBA_BLOCK, :]
            if j == i:
                s = jnp.where(key <= qry, s, NEG_INF)
            elif bias[j] is not None:
                s = s + bias[j]
            s_blocks.append(s)
        m = jnp.max(functools.reduce(jnp.maximum, s_blocks), axis=0, keepdims=True)
        l = None
        for j, s in enumerate(s_blocks):
            p = jnp.exp2(s - m)
            p_scr[j * MOBA_BLOCK:(j + 1) * MOBA_BLOCK, :] = p.astype(BF16)
            p_sum = jnp.sum(p, axis=0, keepdims=True)
            l = p_sum if l is None else l + p_sum
        acc = jnp.dot(vt_scr[:, 0:kv_len], p_scr[0:kv_len, :], preferred_element_type=F32)
        o_ref[rows, :] = (acc / l).T.astype(o_ref.dtype)


def _decode_attend(q, k_own, v_own, k_refs, v_refs, head):
    s_own = jnp.sum(q * k_own, axis=-1, keepdims=True) * ATTN_SCALE
    q_rows = jnp.broadcast_to(q, (HEAD_TILE, HEAD_DIM)).astype(BF16)
    rows = PAGE_SIZE * HEAD_TILE
    lane = lax.broadcasted_iota(jnp.int32, (HEAD_TILE, rows), 1)
    mine = (lane & (HEAD_TILE - 1)) == (head & (HEAD_TILE - 1))
    scores = []
    for k_ref in k_refs:
        k2 = k_ref[...].reshape(rows, HEAD_DIM).astype(BF16)
        s = lax.dot_general(q_rows, k2, (((1,), (1,)), ((), ())), preferred_element_type=F32) * ATTN_SCALE
        scores.append(jnp.where(mine, s, NEG_INF))
    m = s_own
    for s in scores:
        m = jnp.maximum(m, jnp.max(s, axis=-1, keepdims=True))
    l = jnp.exp(s_own - m)
    acc = l * v_own
    for s, v_ref in zip(scores, v_refs):
        p = jnp.exp(s - m)
        l = l + jnp.sum(p, axis=-1, keepdims=True)
        acc = acc + jnp.dot(p.astype(BF16), v_ref[...].reshape(rows, HEAD_DIM).astype(BF16), preferred_element_type=F32)
    return (acc / l)[0:1, :]


def _moba_call(page_table_flat, topk_flat, q, k, v, dq, dk_new, dv_new, cache_k, cache_v, pages_per_seq):
    b_dim, l_dim, hd = q.shape
    s_dim = dq.shape[0]
    n_blocks = l_dim // MOBA_BLOCK
    assert n_blocks <= GATE_ROWS and s_dim % b_dim == 0 and N_HEADS & (N_HEADS - 1) == 0
    pairs = s_dim // b_dim
    head_spec = pl.BlockSpec((None, l_dim, HEAD_DIM), lambda b, h, pt, tk: (b, 0, h))
    dec_spec = pl.BlockSpec((s_dim, N_HEADS, HEAD_DIM), lambda b, h, pt, tk: (0, 0, 0))

    def page_spec(t, j, half):
        def index(b, h, pt, tk):
            pair = (b * N_HEADS + h) * pairs + t
            seq, head = pair // N_HEADS, pair % N_HEADS
            block = tk[pair * MOBA_TOPK + j]
            return (pt[seq * pages_per_seq + PAGES_PER_BLOCK * block + half], 0, head // HEAD_TILE, 0)
        return pl.BlockSpec((None, PAGE_SIZE, HEAD_TILE, HEAD_DIM), index)

    page_specs = [page_spec(t, j, half) for t in range(pairs) for j in range(MOBA_TOPK)
                  for half in range(PAGES_PER_BLOCK)]
    n_pages = len(page_specs)
    vmem = (4 * _nbytes((l_dim, HEAD_DIM), F32) + 6 * _nbytes((l_dim, HEAD_DIM), BF16)
            + _nbytes((GATE_ROWS, HEAD_DIM), F32) + _nbytes((l_dim, MOBA_BLOCK), BF16)
            + 2 * _nbytes((l_dim, MOBA_BLOCK), F32)
            + 4 * n_pages * _nbytes((PAGE_SIZE, HEAD_TILE, HEAD_DIM), F32) + 8 * _nbytes((s_dim, N_HEADS, HEAD_DIM), F32))
    return pl.pallas_call(
        functools.partial(_moba_kernel, n_blocks=n_blocks, pairs=pairs),
        grid_spec=pltpu.PrefetchScalarGridSpec(
            num_scalar_prefetch=2,
            grid=(b_dim, N_HEADS),
            in_specs=[head_spec, head_spec, head_spec, dec_spec, dec_spec, dec_spec] + page_specs + page_specs,
            out_specs=[head_spec, dec_spec],
            scratch_shapes=[pltpu.VMEM((l_dim, HEAD_DIM), BF16), pltpu.VMEM((HEAD_DIM, l_dim), BF16),
                            pltpu.VMEM((GATE_ROWS, HEAD_DIM), F32), pltpu.VMEM((l_dim, MOBA_BLOCK), BF16)],
        ),
        out_shape=[jax.ShapeDtypeStruct((b_dim, l_dim, hd), BF16),
                   jax.ShapeDtypeStruct((s_dim, N_HEADS, HEAD_DIM), F32)],
        compiler_params=_params(2, vmem),
        name="moba_attention",
    )(page_table_flat, topk_flat, q, k, v, dq, dk_new, dv_new, *([cache_k] * n_pages), *([cache_v] * n_pages))


def _topk_kernel(q_ref, km_ref, o_ref):
    n_blocks = km_ref.shape[0]
    km = km_ref[...].reshape(n_blocks * N_HEADS, HEAD_DIM).astype(BF16)
    gate = lax.dot_general(q_ref[...].astype(BF16), km, (((1,), (1,)), ((), ())), preferred_element_type=F32)
    lane = lax.broadcasted_iota(jnp.int32, gate.shape, 1)
    head = lax.broadcasted_iota(jnp.int32, gate.shape, 0)
    gate = jnp.where((lane & (N_HEADS - 1)) == head, gate, -jnp.inf)
    lane_f = lane.astype(F32)
    out_lane = lax.broadcasted_iota(jnp.int32, o_ref.shape, 1)
    out = jnp.zeros(o_ref.shape, jnp.int32)
    for t in range(MOBA_TOPK):
        best = jnp.max(gate, axis=-1, keepdims=True)
        arg = jnp.min(jnp.where(gate == best, lane_f, float(gate.shape[1])), axis=-1, keepdims=True)
        out = jnp.where(out_lane == t, arg.astype(jnp.int32) // N_HEADS, out)
        gate = jnp.where(lane_f == arg, -jnp.inf, gate)
    o_ref[...] = out


def _topk_call(q, kmean):
    s_dim = q.shape[0]
    n_blocks = kmean.shape[1]
    assert N_HEADS & (N_HEADS - 1) == 0
    vmem = 2 * (_nbytes((N_HEADS, HEAD_DIM), F32) + _nbytes((n_blocks, N_HEADS, HEAD_DIM), F32)
                + _nbytes((N_HEADS, GATE_LANES), jnp.int32)) + 8 * _nbytes((N_HEADS, n_blocks * N_HEADS), F32)
    return pl.pallas_call(
        _topk_kernel,
        grid=(s_dim,),
        in_specs=[pl.BlockSpec((None, N_HEADS, HEAD_DIM), lambda b: (b, 0, 0)),
                  pl.BlockSpec((None, n_blocks, N_HEADS, HEAD_DIM), lambda b: (b, 0, 0, 0))],
        out_specs=pl.BlockSpec((None, N_HEADS, GATE_LANES), lambda b: (b, 0, 0)),
        out_shape=jax.ShapeDtypeStruct((s_dim, N_HEADS, GATE_LANES), jnp.int32),
        compiler_params=_params(1, vmem),
        name="moba_decode_topk",
    )(q, kmean)


FFN_BN = 512


def _trunk(x, mods, kv_mod, fin_mod, weights, *, bm, rows_per_group, gate_fn, q_dtype, v_dtype, kmean_job=None):
    (g_mix, g_ffn, w_a_in, b_a_in, ln_a_g, ln_a_b, w_a_out, g_kv, w_k, w_v, w_q, w_o,
     w_ff_gate, w_ff_up, w_ff_down, g_fin) = weights
    m_dim, d = x.shape
    depth = g_mix.shape[0]
    n_a_layers = w_a_in.shape[0]
    d_a = w_a_in.shape[2] // 2
    half_bm = bm // 2 if bm % 16 == 0 else bm
    mmr = functools.partial(_mmr_call, bm=bm, bn=min(1024, d), rows_per_group=rows_per_group)
    ffn_steps = _mmc_steps(m_dim, w_ff_gate.shape[2], bm, FFN_BN)
    a_rows = []
    km_parts = []
    k_new = v_new = None
    for l in range(depth):
        if l == n_a_layers:
            kv_norm = (g_kv,) + tuple(kv_mod)
            k_new = mmr(x, w_k, 0, "plain", norm=kv_norm, out_dtype=F32)
            v_new = mmr(x, w_v, 0, "plain", norm=kv_norm, out_dtype=F32)
        sh1, sc1, g1, sh2, sc2, g2 = mods[l]
        mix_norm = (g_mix[l:l + 1], sh1, sc1)
        if l < n_a_layers:
            bias = b_a_in[l:l + 1]
            u = mmr(x, w_a_in, l, "gelu", norm=mix_norm, bias=bias, n_cols=d_a, out_dtype=BF16)
            v_ln = _mmr_call(x, w_a_in, l, "gelu_ln", bm=half_bm, bn=min(1024, d), rows_per_group=rows_per_group,
                             norm=mix_norm, bias=bias, ln=(ln_a_g[l:l + 1], ln_a_b[l:l + 1]),
                             n_cols=d_a, col_off=d_a, out_dtype=v_dtype)
            a_rows.append(v_ln)
            x = mmr(gate_fn(l, u, v_ln), w_a_out, l, "residual", res=x, gate=g1, out_dtype=F32)
        else:
            j = l - n_a_layers
            q = mmr(x, w_q, j, "plain", norm=mix_norm, out_dtype=q_dtype)
            means = jnp.concatenate(km_parts, axis=0) if km_parts else None
            o = yield q, k_new, v_new, means
            km_parts = []
            x = mmr(o, w_o, j, "residual", res=x, gate=g1, out_dtype=F32)
        h = _norm_call(x, g_ffn[l:l + 1], sh2, sc2, bm=min(bm, 512), rows_per_group=rows_per_group, out_dtype=BF16)
        if kmean_job is not None and l < n_a_layers:
            page_table_flat, cache_k, blocks_per_step = kmean_job
            a, part = _mmc_call(h, [w_ff_gate, w_ff_up], l, "swiglu", bm=bm, bn=FFN_BN,
                                kmean=(page_table_flat, cache_k, l * ffn_steps * blocks_per_step, blocks_per_step))
            km_parts.append(part)
        else:
            a = _mmc_call(h, [w_ff_gate, w_ff_up], l, "swiglu", bm=bm, bn=FFN_BN)
        x = _mmc_call(a, [w_ff_down], l, "residual", bm=bm, bn=FFN_BN, k_steps=2, out_dtype=F32, res=x, gate=g2,
                      rows_per_group=rows_per_group)
    f_shift, f_scale = fin_mod
    y = _norm_call(x, g_fin, f_shift, f_scale, bm=min(bm, 512), rows_per_group=rows_per_group, out_dtype=F32)
    return y, k_new, v_new, a_rows


def _send(trunk, value):
    try:
        return trunk.send(value), None
    except StopIteration as done:
        return None, done.value


def kernel(x_prompt, x_sample, c_prompt, c_sample, cache_k, cache_v, page_table, w_ada, b_ada, g_mix, g_ffn, w_a_in, b_a_in, ln_a_g, ln_a_b, w_s, b_s, w_a_out, w_kv_ada, b_kv_ada, g_kv, w_k, w_v, w_q, w_o, w_ff_gate, w_ff_up, w_ff_down, w_fin_ada, b_fin_ada, g_fin):
    batch, seq, d = x_prompt.shape
    n_seq, dec_seq, _ = x_sample.shape
    assert dec_seq == 1 and batch + n_seq <= C_ROWS
    depth = w_ada.shape[0]
    n_pages, page_size, n_heads, head_dim = cache_k.shape
    assert (page_size, n_heads, head_dim) == (PAGE_SIZE, N_HEADS, HEAD_DIM)
    pages_per_seq = page_table.shape[1]
    n_past_blocks = pages_per_seq // PAGES_PER_BLOCK

    c_rows = jnp.concatenate([c_prompt, c_sample, jnp.zeros((C_ROWS - batch - n_seq, d), F32)], axis=0)
    mods = _ada_call(c_rows, w_ada, b_ada[:, None, :])
    kv_mod = _ada_call(c_rows, w_kv_ada[None], b_kv_ada[None, None, :])[0]
    fin_mod = _ada_call(c_rows, w_fin_ada[None], b_fin_ada[None, None, :])[0]

    def prompt_vecs(rows, n):
        return [rows[:batch, i * d:(i + 1) * d][:, None, :] for i in range(n)]

    def sample_vecs(rows, n):
        return [rows[batch:batch + n_seq, i * d:(i + 1) * d][None] for i in range(n)]

    weights = (g_mix, g_ffn, w_a_in, b_a_in, ln_a_g, ln_a_b, w_a_out, g_kv[None], w_k[None], w_v[None], w_q, w_o,
               w_ff_gate, w_ff_up, w_ff_down, g_fin[None])
    b_s_t = jnp.swapaxes(b_s, 1, 2)

    pt_flat = page_table.reshape(-1)
    n_a_layers = w_a_in.shape[0]
    group_w = d // A_GROUPS
    heads = (n_seq, N_HEADS, HEAD_DIM)
    bm_prompt = min(1024, seq)

    km_slots = n_a_layers * _mmc_steps(batch * seq, w_ff_gate.shape[2], bm_prompt, FFN_BN)
    km_blocks_per_step = -(-(n_seq * n_past_blocks) // km_slots)

    def prompt_gate(l, u, v_ln):
        return _gmlp_gate_call(u, v_ln, w_s[l], b_s_t[l], rows=512)

    def sample_gate(l, u, v_ln):
        w00 = jnp.repeat(w_s[l, :, 0, 0], group_w)[None, :]
        b0 = jnp.repeat(b_s[l, :, 0], group_w)[None, :]
        return _gmlp_gate_row0_call(u, v_ln, w00, b0)

    prompt = _trunk(
        x_prompt.reshape(batch * seq, d),
        [prompt_vecs(mods[l], 6) for l in range(depth)], prompt_vecs(kv_mod, 2), prompt_vecs(fin_mod, 2),
        weights, bm=bm_prompt, rows_per_group=seq, gate_fn=prompt_gate, q_dtype=BF16, v_dtype=BF16,
        kmean_job=(pt_flat, cache_k, km_blocks_per_step))
    sample = _trunk(
        x_sample.reshape(n_seq, d),
        [sample_vecs(mods[l], 6) for l in range(depth)], sample_vecs(kv_mod, 2), sample_vecs(fin_mod, 2),
        weights, bm=n_seq, rows_per_group=n_seq, gate_fn=sample_gate, q_dtype=F32, v_dtype=F32)

    p_req, s_req = next(prompt), next(sample)
    kmean = p_req[3][:n_seq * n_past_blocks].reshape(n_seq, n_past_blocks, N_HEADS, HEAD_DIM)
    p_out = s_out = None
    for _ in range(depth - n_a_layers):
        (q_p, k_p, v_p, _), (q_s, k_s, v_s, _) = p_req, s_req
        q_s = q_s.reshape(heads)
        topk = _topk_call(q_s, kmean)[:, :, :MOBA_TOPK].reshape(-1)
        o_p, o_s = _moba_call(pt_flat, topk, q_p.reshape(batch, seq, d), k_p.reshape(batch, seq, d),
                              v_p.reshape(batch, seq, d), q_s, k_s.reshape(heads), v_s.reshape(heads),
                              cache_k, cache_v, pages_per_seq)
        p_req, p_out = _send(prompt, o_p.reshape(batch * seq, d))
        s_req, s_out = _send(sample, o_s.reshape(n_seq, d).astype(BF16))
    y_p, k_p, v_p, _ = p_out
    y_s, k_s, v_s, a_rows = s_out

    state_a_v = jnp.stack(a_rows, axis=0).reshape(len(a_rows), n_seq, 1, d)
    return (y_p.reshape(batch, seq, d), y_s.reshape(n_seq, 1, d),
            k_p.reshape(batch, seq, N_HEADS, HEAD_DIM), v_p.reshape(batch, seq, N_HEADS, HEAD_DIM),
            k_s.reshape(n_seq, 1, N_HEADS, HEAD_DIM), v_s.reshape(n_seq, 1, N_HEADS, HEAD_DIM),
            state_a_v)
```

```python
import functools

import jax
import jax.numpy as jnp
from jax import lax
from jax.experimental import pallas as pl
from jax.experimental.pallas import tpu as pltpu

F32 = jnp.float32
BF16 = jnp.bfloat16

N_HEADS = 16
HEAD_DIM = 128
CHUNK = 128
A_GROUPS = 16
MOBA_BLOCK = 256
MOBA_TOPK = 3
PAGE_SIZE = 128
PAGES_PER_BLOCK = MOBA_BLOCK // PAGE_SIZE
RMS_EPS = 1e-6
LN_EPS = 1e-5
NEG_INF = -1e30
ATTN_SCALE = HEAD_DIM ** -0.5
GATE_LANES = 128

C_ROWS = 16
MIB = 1024 * 1024
VMEM_LIMIT_V7X = 58 * MIB
COMPILER_TEMP_BYTES = 16 * MIB


def _params(n_axes, vmem_bytes):
    limit = min(int(vmem_bytes) + COMPILER_TEMP_BYTES, VMEM_LIMIT_V7X)
    return pltpu.CompilerParams(dimension_semantics=("arbitrary",) * n_axes,
                                vmem_limit_bytes=limit)


def _nbytes(shape, dtype):
    n = 1
    for s in shape:
        n *= s
    return n * jnp.dtype(dtype).itemsize


def _ada_kernel(c_ref, w_ref, b_ref, o_ref):
    sc = jax.nn.silu(c_ref[...]).astype(BF16)
    o_ref[...] = jnp.dot(sc, w_ref[...].astype(BF16), preferred_element_type=F32) + b_ref[...]


def _ada_call(c_rows, w, b):
    layers, k_dim, n_dim = w.shape
    tn = 1024
    vmem = 2 * (_nbytes((k_dim, tn), F32) + _nbytes((C_ROWS, k_dim), F32) + 2 * _nbytes((C_ROWS, tn), F32))
    return pl.pallas_call(
        _ada_kernel,
        grid=(layers, n_dim // tn),
        in_specs=[pl.BlockSpec((C_ROWS, k_dim), lambda l, n: (0, 0)),
                  pl.BlockSpec((None, k_dim, tn), lambda l, n: (l, 0, n)),
                  pl.BlockSpec((None, 1, tn), lambda l, n: (l, 0, n))],
        out_specs=pl.BlockSpec((None, C_ROWS, tn), lambda l, n: (l, 0, n)),
        out_shape=jax.ShapeDtypeStruct((layers, C_ROWS, n_dim), F32),
        compiler_params=_params(2, vmem),
        name="ada_modulation",
    )(c_rows, w, b)


def _norm_kernel(x_ref, g_ref, sh_ref, sc_ref, o_ref):
    x = x_ref[...]
    y = x * lax.rsqrt(jnp.mean(x * x, axis=-1, keepdims=True) + RMS_EPS) * g_ref[...]
    o_ref[...] = (y * (1.0 + sc_ref[...]) + sh_ref[...]).astype(o_ref.dtype)


def _norm_call(x, g, shift, scale, bm, rows_per_group, out_dtype):
    m_dim, d = x.shape
    r = shift.shape[1]
    blocks_per_group = rows_per_group // bm
    mod_spec = pl.BlockSpec((None, r, d), lambda m: (m // blocks_per_group, 0, 0))
    vmem = 2 * (_nbytes((bm, d), F32) + _nbytes((bm, d), out_dtype) + 3 * _nbytes((8, d), F32)) + 2 * _nbytes((bm, d), F32)
    return pl.pallas_call(
        _norm_kernel,
        grid=(m_dim // bm,),
        in_specs=[pl.BlockSpec((bm, d), lambda m: (m, 0)),
                  pl.BlockSpec((1, d), lambda m: (0, 0)),
                  mod_spec, mod_spec],
        out_specs=pl.BlockSpec((bm, d), lambda m: (m, 0)),
        out_shape=jax.ShapeDtypeStruct((m_dim, d), out_dtype),
        compiler_params=_params(1, vmem),
        name="rmsnorm_modulate",
    )(x, g, shift, scale)


def _mmc_kernel(*refs, n_w, epi, k_steps, km_blocks, side):
    if km_blocks:
        refs = refs[1:]
    refs = list(refs)

    def take(count):
        taken = refs[:count]
        del refs[:count]
        return taken

    n_vec = 2 if epi == "residual" else 0
    (x_ref,), w_refs, main_vecs = take(1), take(n_w), take(n_vec)
    side_x, side_vecs = (take(1), take(n_vec)) if side else ((), ())
    page_refs = take(km_blocks * PAGES_PER_BLOCK)
    (o_ref,) = take(1)
    km_out = take(1 if km_blocks else 0)
    side_o = take(1 if side else 0)
    wb_refs = take(n_w)
    acc_refs = take(n_w if k_steps > 1 else 0)
    side_acc_refs = take(n_w if side and k_steps > 1 else 0)
    m = pl.program_id(1)
    k = pl.program_id(2)

    def finish(out_ref, vecs, accs):
        if epi == "swiglu":
            out_ref[...] = (jax.nn.silu(accs[0]) * accs[1]).astype(out_ref.dtype)
        elif epi == "residual":
            res_ref, gate_ref = vecs
            out_ref[...] = res_ref[...] + gate_ref[...] * accs[0]

    def row_group(rows_ref, out_ref, vecs, accs_refs):
        rows = rows_ref[...]
        parts = [jnp.dot(rows, wb_ref[k], preferred_element_type=F32) for wb_ref in wb_refs]
        if k_steps == 1:
            finish(out_ref, vecs, parts)
            return

        @pl.when(k == 0)
        def _first():
            for a_ref, p in zip(accs_refs, parts):
                a_ref[...] = p

        if k_steps > 2:
            @pl.when((k > 0) & (k < k_steps - 1))
            def _accumulate():
                for a_ref, p in zip(accs_refs, parts):
                    a_ref[...] += p

        @pl.when(k == k_steps - 1)
        def _last():
            finish(out_ref, vecs, [a_ref[...] + p for a_ref, p in zip(accs_refs, parts)])

    @pl.when(m == 0)
    def _first_row_block():
        for w_ref, wb_ref in zip(w_refs, wb_refs):
            wb_ref[k] = w_ref[...].astype(BF16)
        if side:
            row_group(side_x[0], side_o[0], side_vecs, side_acc_refs)

    row_group(x_ref, o_ref, main_vecs, acc_refs)
    for t in range(km_blocks):
        pages = page_refs[t * PAGES_PER_BLOCK:(t + 1) * PAGES_PER_BLOCK]
        total = functools.reduce(jnp.add, [jnp.sum(p[...], axis=0) for p in pages])
        km_out[0][t] = total * (1.0 / MOBA_BLOCK)


def _mmc_steps(m_dim, n_dim, bm, bn):
    return (n_dim // bn) * (m_dim // bm)


def _mmc_call(x, ws, layer, epi, *, bm, bn, k_steps=1, out_dtype=BF16, res=None, gate=None, rows_per_group=None,
              side=None, kmean=None):
    m_dim, k_dim = x.shape
    n_dim = ws[0].shape[2]
    n_w = len(ws)
    bk = k_dim // k_steps
    m_steps = m_dim // bm
    grid = (n_dim // bn, m_steps, k_steps)
    last_k = k_steps - 1

    def w_index(n, m, k, *_):
        return (layer, jnp.where(m == 0, k, last_k), n)

    in_specs = [pl.BlockSpec((bm, bk), lambda n, m, k, *_: (m, k))]
    in_specs += [pl.BlockSpec((None, bk, bn), w_index) for _ in ws]
    args = [x] + list(ws)
    vmem = 2 * _nbytes((bm, bk), x.dtype) + n_w * 2 * _nbytes((bk, bn), F32)
    if epi == "residual":
        r = gate.shape[1]
        blocks_per_group = rows_per_group // bm
        in_specs.append(pl.BlockSpec((bm, bn), lambda n, m, k, *_: (m, n)))
        in_specs.append(pl.BlockSpec((None, r, bn), lambda n, m, k, *_: (m // blocks_per_group, 0, n)))
        args += [res, gate]
        vmem += 2 * _nbytes((bm, bn), F32) + 2 * _nbytes((max(r, 8), bn), F32)
    if side is not None:
        s_dim = side[0].shape[0]
        in_specs.append(pl.BlockSpec((s_dim, bk), lambda n, m, k, *_: (0, k)))
        if epi == "residual":
            in_specs.append(pl.BlockSpec((s_dim, bn), lambda n, m, k, *_: (0, n)))
            in_specs.append(pl.BlockSpec((None, s_dim, bn), lambda n, m, k, *_: (0, 0, n)))
        args += list(side)
        vmem += 2 * _nbytes((16, bk), BF16) + 8 * _nbytes((16, bn), F32)
    out_specs = [pl.BlockSpec((bm, bn), lambda n, m, k, *_: (m, n))]
    out_shape = [jax.ShapeDtypeStruct((m_dim, n_dim), out_dtype)]
    vmem += 2 * _nbytes((bm, bn), out_dtype)
    km_blocks = 0
    if kmean is not None:
        assert k_steps == 1
        page_table_flat, cache_k, first_block, km_blocks = kmean
        last_block = (page_table_flat.shape[0] // PAGES_PER_BLOCK) - 1
        page_shape = (PAGE_SIZE, N_HEADS, HEAD_DIM)

        def page_spec(t, half):
            def index(n, m, k, pt):
                g = jnp.minimum(first_block + (n * m_steps + m) * km_blocks + t, last_block)
                return (pt[g * PAGES_PER_BLOCK + half], 0, 0, 0)
            return pl.BlockSpec((None,) + page_shape, index)

        in_specs += [page_spec(t, half) for t in range(km_blocks) for half in range(PAGES_PER_BLOCK)]
        args += [cache_k] * (km_blocks * PAGES_PER_BLOCK)
        out_specs.append(pl.BlockSpec((km_blocks, N_HEADS, HEAD_DIM), lambda n, m, k, pt: (n * m_steps + m, 0, 0)))
        out_shape.append(jax.ShapeDtypeStruct((grid[0] * m_steps * km_blocks, N_HEADS, HEAD_DIM), F32))
        vmem += 2 * km_blocks * PAGES_PER_BLOCK * _nbytes(page_shape, F32) + 2 * km_blocks * _nbytes((N_HEADS, HEAD_DIM), F32)
        args = [page_table_flat] + args
    if side is not None:
        out_specs.append(pl.BlockSpec((s_dim, bn), lambda n, m, k, *_: (0, n)))
        out_shape.append(jax.ShapeDtypeStruct((s_dim, n_dim), out_dtype))
    scratch = [pltpu.VMEM((k_steps, bk, bn), BF16) for _ in ws]
    vmem += n_w * _nbytes((k_steps, bk, bn), BF16)
    if k_steps > 1:
        scratch += [pltpu.VMEM((bm, bn), F32) for _ in ws]
        vmem += n_w * _nbytes((bm, bn), F32)
        if side is not None:
            scratch += [pltpu.VMEM((s_dim, bn), F32) for _ in ws]
    outs = pl.pallas_call(
        functools.partial(_mmc_kernel, n_w=n_w, epi=epi, k_steps=k_steps, km_blocks=km_blocks,
                          side=side is not None),
        grid_spec=pltpu.PrefetchScalarGridSpec(
            num_scalar_prefetch=1 if kmean is not None else 0,
            grid=grid,
            in_specs=in_specs,
            out_specs=out_specs,
            scratch_shapes=scratch,
        ),
        out_shape=out_shape,
        compiler_params=_params(3, vmem),
        name="mmc_" + epi,
    )(*args)
    return list(outs)


def _mmr_kernel(*refs, epi, norm, n_steps, side):
    refs = list(refs)

    def take(count):
        taken = refs[:count]
        del refs[:count]
        return taken

    n_mod = 2 if norm else 0
    n_vec = 2 if epi == "residual" else 0
    (x_ref,), g_ref, main_mod = take(1), take(1 if norm else 0), take(n_mod)
    (w_ref,) = take(1)
    bias = take(1 if epi in ("gelu", "gelu_ln") else 0)
    ln = take(2 if epi == "gelu_ln" else 0)
    main_vecs = take(n_vec)
    side_x, side_mod, side_vecs = (take(1), take(n_mod), take(n_vec)) if side else ((), (), ())
    (o_ref,) = take(1)
    side_o = take(1 if side else 0)
    (wb_scr,) = take(1)
    h_scr = take(1 if norm else 0)
    z_scr = take(1 if epi == "gelu_ln" else 0)
    side_h = take(1 if side and norm else 0)
    side_z = take(1 if side and epi == "gelu_ln" else 0)
    m = pl.program_id(0)
    n = pl.program_id(1)

    def row_group(rows_ref, mod, h, out_ref, vecs, z):
        if norm:
            @pl.when(n == 0)
            def _normalize():
                sh_ref, sc_ref = mod
                x = rows_ref[...]
                y = x * lax.rsqrt(jnp.mean(x * x, axis=-1, keepdims=True) + RMS_EPS) * g_ref[0][...]
                h[0][...] = (y * (1.0 + sc_ref[...]) + sh_ref[...]).astype(BF16)
            rows = h[0][...]
        else:
            rows = rows_ref[...]
        acc = jnp.dot(rows, wb_scr[n], preferred_element_type=F32)
        if epi == "plain":
            out_ref[...] = acc.astype(out_ref.dtype)
        elif epi == "gelu":
            out_ref[...] = jax.nn.gelu(acc + bias[0][...]).astype(out_ref.dtype)
        elif epi == "residual":
            res_ref, gate_ref = vecs
            out_ref[...] = res_ref[...] + gate_ref[...] * acc
        elif epi == "gelu_ln":
            z[0][n] = jax.nn.gelu(acc + bias[0][...])

            @pl.when(n == n_steps - 1)
            def _layernorm():
                lng_ref, lnb_ref = ln
                bn = z[0].shape[2]
                inv_n = 1.0 / (n_steps * bn)
                zs = [z[0][j] for j in range(n_steps)]
                mu = sum(jnp.sum(zj, axis=-1, keepdims=True) for zj in zs) * inv_n
                var = sum(jnp.sum(jnp.square(zj - mu), axis=-1, keepdims=True) for zj in zs) * inv_n
                rstd = lax.rsqrt(var + LN_EPS)
                for j, zj in enumerate(zs):
                    cols = slice(j * bn, (j + 1) * bn)
                    out_ref[:, cols] = ((zj - mu) * rstd * lng_ref[:, cols] + lnb_ref[:, cols]).astype(out_ref.dtype)

    @pl.when(m == 0)
    def _first_row_block():
        wb_scr[n] = w_ref[...].astype(BF16)
        if side:
            row_group(side_x[0], side_mod, side_h, side_o[0], side_vecs, side_z)

    row_group(x_ref, main_mod, h_scr, o_ref, main_vecs, z_scr)


def _mmr_call(x, w, layer, epi, *, bm, bn, rows_per_group, out_dtype, n_cols=None, col_off=0, norm=None,
              bias=None, ln=None, res=None, gate=None, side=None):
    m_dim, k_dim = x.shape
    n_dim = w.shape[2] if n_cols is None else n_cols
    n_steps = n_dim // bn
    m_steps = m_dim // bm
    off = col_off // bn
    blocks_per_group = rows_per_group // bm

    def first_pass(m, n):
        return jnp.where(m == 0, n, n_steps - 1)

    def w_index(m, n):
        return (layer, 0, off + first_pass(m, n))

    w_buffers = 1 if m_steps > 1 else 2
    in_specs = [pl.BlockSpec((bm, k_dim), lambda m, n: (m, 0))]
    args = [x]
    vmem = 2 * _nbytes((bm, k_dim), x.dtype) + w_buffers * _nbytes((k_dim, bn), F32) + _nbytes((k_dim, n_dim), BF16)
    scratch = [pltpu.VMEM((n_steps, k_dim, bn), BF16)]
    if norm is not None:
        g, shift, scale = norm
        r = shift.shape[1]
        vec_spec = pl.BlockSpec((None, r, k_dim), lambda m, n: (m // blocks_per_group, 0, 0))
        in_specs += [pl.BlockSpec((1, k_dim), lambda m, n: (0, 0)), vec_spec, vec_spec]
        args += [g, shift, scale]
        scratch.append(pltpu.VMEM((bm, k_dim), BF16))
        vmem += 6 * _nbytes((max(r, 8), k_dim), F32) + _nbytes((bm, k_dim), BF16)
    in_specs.append(pl.BlockSpec((None, k_dim, bn), w_index, pipeline_mode=pl.Buffered(w_buffers)))
    args.append(w)
    if epi in ("gelu", "gelu_ln"):
        in_specs.append(pl.BlockSpec((1, bn), lambda m, n: (0, off + n)))
        args.append(bias)
        vmem += 2 * _nbytes((8, bn), F32)
    if epi == "gelu_ln":
        in_specs += [pl.BlockSpec((1, n_dim), lambda m, n: (0, 0))] * 2
        args += list(ln)
        scratch.append(pltpu.VMEM((n_steps, bm, bn), F32))
        vmem += 4 * _nbytes((8, n_dim), F32) + _nbytes((bm, n_dim), F32)
    if epi == "residual":
        r = gate.shape[1]
        in_specs.append(pl.BlockSpec((bm, bn), lambda m, n: (m, n)))
        in_specs.append(pl.BlockSpec((None, r, bn), lambda m, n: (m // blocks_per_group, 0, n)))
        args += [res, gate]
        vmem += 2 * _nbytes((bm, bn), F32) + 2 * _nbytes((max(r, 8), bn), F32)
    if epi == "gelu_ln":
        out_specs = [pl.BlockSpec((bm, n_dim), lambda m, n: (m, 0))]
        vmem += 2 * _nbytes((bm, n_dim), out_dtype)
    else:
        out_specs = [pl.BlockSpec((bm, bn), lambda m, n: (m, n))]
        vmem += 2 * _nbytes((bm, bn), out_dtype)
    out_shape = [jax.ShapeDtypeStruct((m_dim, n_dim), out_dtype)]
    if side is not None:
        s_dim = side["x"].shape[0]
        in_specs.append(pl.BlockSpec((s_dim, k_dim), lambda m, n: (0, 0)))
        args.append(side["x"])
        if norm is not None:
            in_specs += [pl.BlockSpec((None, s_dim, k_dim), lambda m, n: (0, 0, 0))] * 2
            args += list(side["norm"][1:])
            scratch.append(pltpu.VMEM((s_dim, k_dim), BF16))
        if epi == "residual":
            in_specs.append(pl.BlockSpec((s_dim, bn), lambda m, n: (0, first_pass(m, n))))
            in_specs.append(pl.BlockSpec((None, s_dim, bn), lambda m, n: (0, 0, first_pass(m, n))))
            args += [side["res"], side["gate"]]
        if epi == "gelu_ln":
            out_specs.append(pl.BlockSpec((s_dim, n_dim), lambda m, n: (0, 0)))
            scratch.append(pltpu.VMEM((n_steps, s_dim, bn), F32))
        else:
            out_specs.append(pl.BlockSpec((s_dim, bn), lambda m, n: (0, first_pass(m, n))))
        out_shape.append(jax.ShapeDtypeStruct((s_dim, n_dim), side["out_dtype"]))
        vmem += 16 * _nbytes((16, k_dim), F32) + 8 * _nbytes((16, n_dim), F32)
    outs = pl.pallas_call(
        functools.partial(_mmr_kernel, epi=epi, norm=norm is not None, n_steps=n_steps, side=side is not None),
        grid=(m_steps, n_steps),
        in_specs=in_specs,
        out_specs=out_specs,
        out_shape=out_shape,
        scratch_shapes=scratch,
        compiler_params=_params(2, vmem),
        name="mmr_" + epi,
    )(*args)
    return outs[0] if side is None else tuple(outs)


def _gmlp_gate_kernel(u_ref, v_ref, ws_ref, bst_ref, o_ref, wm_scr, *, rows):
    group_w = u_ref.shape[1] // A_GROUPS

    @pl.when(pl.program_id(0) == 0)
    def _mask_weights():
        t = lax.broadcasted_iota(jnp.int32, (CHUNK, CHUNK), 0)
        s = lax.broadcasted_iota(jnp.int32, (CHUNK, CHUNK), 1)
        for g in range(A_GROUPS):
            wm_scr[g] = jnp.where(s <= t, ws_ref[g], 0.0).astype(BF16)

    for c in range(rows // CHUNK):
        chunk = slice(c * CHUNK, (c + 1) * CHUNK)
        for g in range(A_GROUPS):
            cols = slice(g * group_w, (g + 1) * group_w)
            mixed = jnp.dot(wm_scr[g], v_ref[chunk, cols], preferred_element_type=F32) + bst_ref[:, g:g + 1]
            o_ref[chunk, cols] = (u_ref[chunk, cols] * mixed).astype(o_ref.dtype)


def _gmlp_gate_call(u, v, w_s, b_s_t, rows):
    m_dim, d_a = u.shape
    row_spec = pl.BlockSpec((rows, d_a), lambda m: (m, 0))
    vmem = (6 * _nbytes((rows, d_a), BF16) + 3 * _nbytes((A_GROUPS, CHUNK, CHUNK), F32)
            + 2 * _nbytes((CHUNK, 128), F32))
    return pl.pallas_call(
        functools.partial(_gmlp_gate_kernel, rows=rows),
        grid=(m_dim // rows,),
        in_specs=[row_spec, row_spec,
                  pl.BlockSpec((A_GROUPS, CHUNK, CHUNK), lambda m: (0, 0, 0)),
                  pl.BlockSpec((CHUNK, A_GROUPS), lambda m: (0, 0))],
        out_specs=row_spec,
        out_shape=jax.ShapeDtypeStruct((m_dim, d_a), BF16),
        scratch_shapes=[pltpu.VMEM((A_GROUPS, CHUNK, CHUNK), BF16)],
        compiler_params=_params(1, vmem),
        name="gmlp_spatial_gate",
    )(u, v, w_s, b_s_t)


def _gmlp_gate_row0_kernel(u_ref, v_ref, w00_ref, b0_ref, o_ref):
    o_ref[...] = (u_ref[...] * (w00_ref[...] * v_ref[...] + b0_ref[...])).astype(o_ref.dtype)


def _gmlp_gate_row0_call(u, v, w00, b0):
    s_dim, d_a = u.shape
    full = lambda shape: pl.BlockSpec(shape, lambda i: (0,) * len(shape))
    vmem = 2 * (3 * _nbytes((max(s_dim, 16), d_a), F32) + 2 * _nbytes((8, d_a), F32))
    return pl.pallas_call(
        _gmlp_gate_row0_kernel,
        grid=(1,),
        in_specs=[full((s_dim, d_a)), full((s_dim, d_a)), full((1, d_a)), full((1, d_a))],
        out_specs=full((s_dim, d_a)),
        out_shape=jax.ShapeDtypeStruct((s_dim, d_a), BF16),
        compiler_params=_params(1, vmem),
        name="gmlp_gate_single_token",
    )(u, v, w00, b0)


GATE_ROWS = 16
LOG2_E = 1.4426950408889634


HEAD_TILE = 8
N_PICKED_PAGES = MOBA_TOPK * PAGES_PER_BLOCK


def _moba_kernel(pt_ref, tk_ref, q_ref, k_ref, v_ref, dq_ref, dkn_ref, dvn_ref, *refs, n_blocks, pairs):
    del pt_ref, tk_ref
    n_pages = pairs * N_PICKED_PAGES
    dk_refs, dv_refs = refs[:n_pages], refs[n_pages:2 * n_pages]
    o_ref, do_ref, kb_scr, vt_scr, km_scr, p_scr = refs[2 * n_pages:]
    _moba_prefill(q_ref, k_ref, v_ref, o_ref, kb_scr, vt_scr, km_scr, p_scr, n_blocks)
    first = (pl.program_id(0) * N_HEADS + pl.program_id(1)) * pairs
    for t in range(pairs):
        pair = first + t
        seq, head = pair // N_HEADS, pair % N_HEADS
        picked = slice(t * N_PICKED_PAGES, (t + 1) * N_PICKED_PAGES)
        do_ref[seq, pl.ds(head, 1), :] = _decode_attend(
            dq_ref[seq, pl.ds(head, 1), :], dkn_ref[seq, pl.ds(head, 1), :], dvn_ref[seq, pl.ds(head, 1), :],
            dk_refs[picked], dv_refs[picked], head)


def _moba_prefill(q_ref, k_ref, v_ref, o_ref, kb_scr, vt_scr, km_scr, p_scr, n_blocks):
    contract_last = (((1,), (1,)), ((), ()))
    kb_scr[...] = k_ref[...].astype(BF16)
    vt_scr[...] = v_ref[...].T.astype(BF16)
    km_scr[...] = jnp.zeros_like(km_scr)
    for n in range(n_blocks):
        km_scr[n:n + 1, :] = jnp.mean(k_ref[n * MOBA_BLOCK:(n + 1) * MOBA_BLOCK, :], axis=0, keepdims=True)
    gate_all = lax.dot_general(km_scr[...].astype(BF16), q_ref[...], contract_last,
                               preferred_element_type=F32)
    blk = lax.broadcasted_iota(jnp.int32, (GATE_ROWS, MOBA_BLOCK), 0)
    key = lax.broadcasted_iota(jnp.int32, (MOBA_BLOCK, MOBA_BLOCK), 0)
    qry = lax.broadcasted_iota(jnp.int32, (MOBA_BLOCK, MOBA_BLOCK), 1)

    for i in range(n_blocks):
        rows = slice(i * MOBA_BLOCK, (i + 1) * MOBA_BLOCK)
        q_t = q_ref[rows, :]
        bias = [None] * i
        if i > MOBA_TOPK:
            gate = jnp.where(blk < i, gate_all[:, rows], NEG_INF)
            for n in range(i):
                g_n = gate[n:n + 1, :]
                beats = (gate > g_n) | ((gate == g_n) & (blk < n))
                cnt = jnp.sum(jnp.where(beats, 1.0, 0.0), axis=0, keepdims=True)
                bias[n] = jnp.where(cnt < MOBA_TOPK, 0.0, NEG_INF)
        kv_len = (i + 1) * MOBA_BLOCK
        s_all = lax.dot_general(kb_scr[0:kv_len, :], q_t, contract_last,
                                preferred_element_type=F32) * (ATTN_SCALE * LOG2_E)
        s_blocks = []
        for j in range(i + 1):
            s = s_all[j * MOBA_BLOCK:(j + 1) * MOBA_BLOCK, :]
            if j == i:
                s = jnp.where(key <= qry, s, NEG_INF)
            elif bias[j] is not None:
                s = s + bias[j]
            s_blocks.append(s)
        m = jnp.max(functools.reduce(jnp.maximum, s_blocks), axis=0, keepdims=True)
        l = None
        for j, s in enumerate(s_blocks):
            p = jnp.exp2(s - m)
            p_scr[j * MOBA_BLOCK:(j + 1) * MOBA_BLOCK, :] = p.astype(BF16)
            p_sum = jnp.sum(p, axis=0, keepdims=True)
            l = p_sum if l is None else l + p_sum
        acc = jnp.dot(vt_scr[:, 0:kv_len], p_scr[0:kv_len, :], preferred_element_type=F32)
        o_ref[rows, :] = (acc / l).T.astype(o_ref.dtype)


def _decode_attend(q, k_own, v_own, k_refs, v_refs, head):
    s_own = jnp.sum(q * k_own, axis=-1, keepdims=True) * ATTN_SCALE
    q_rows = jnp.broadcast_to(q, (HEAD_TILE, HEAD_DIM)).astype(BF16)
    rows = PAGE_SIZE * HEAD_TILE
    lane = lax.broadcasted_iota(jnp.int32, (HEAD_TILE, rows), 1)
    mine = (lane & (HEAD_TILE - 1)) == (head & (HEAD_TILE - 1))
    scores = []
    for k_ref in k_refs:
        k2 = k_ref[...].reshape(rows, HEAD_DIM).astype(BF16)
        s = lax.dot_general(q_rows, k2, (((1,), (1,)), ((), ())), preferred_element_type=F32) * ATTN_SCALE
        scores.append(jnp.where(mine, s, NEG_INF))
    m = s_own
    for s in scores:
        m = jnp.maximum(m, jnp.max(s, axis=-1, keepdims=True))
    l = jnp.exp(s_own - m)
    acc = l * v_own
    for s, v_ref in zip(scores, v_refs):
        p = jnp.exp(s - m)
        l = l + jnp.sum(p, axis=-1, keepdims=True)
        acc = acc + jnp.dot(p.astype(BF16), v_ref[...].reshape(rows, HEAD_DIM).astype(BF16), preferred_element_type=F32)
    return (acc / l)[0:1, :]


def _moba_call(page_table_flat, topk_flat, q, k, v, dq, dk_new, dv_new, cache_k, cache_v, pages_per_seq):
    b_dim, l_dim, hd = q.shape
    s_dim = dq.shape[0]
    n_blocks = l_dim // MOBA_BLOCK
    assert n_blocks <= GATE_ROWS and s_dim % b_dim == 0 and N_HEADS & (N_HEADS - 1) == 0
    pairs = s_dim // b_dim
    head_spec = pl.BlockSpec((None, l_dim, HEAD_DIM), lambda b, h, pt, tk: (b, 0, h))
    dec_spec = pl.BlockSpec((s_dim, N_HEADS, HEAD_DIM), lambda b, h, pt, tk: (0, 0, 0))

    def page_spec(t, j, half):
        def index(b, h, pt, tk):
            pair = (b * N_HEADS + h) * pairs + t
            seq, head = pair // N_HEADS, pair % N_HEADS
            block = tk[pair * MOBA_TOPK + j]
            return (pt[seq * pages_per_seq + PAGES_PER_BLOCK * block + half], 0, head // HEAD_TILE, 0)
        return pl.BlockSpec((None, PAGE_SIZE, HEAD_TILE, HEAD_DIM), index)

    page_specs = [page_spec(t, j, half) for t in range(pairs) for j in range(MOBA_TOPK)
                  for half in range(PAGES_PER_BLOCK)]
    n_pages = len(page_specs)
    vmem = (4 * _nbytes((l_dim, HEAD_DIM), F32) + 6 * _nbytes((l_dim, HEAD_DIM), BF16)
            + _nbytes((GATE_ROWS, HEAD_DIM), F32) + _nbytes((l_dim, MOBA_BLOCK), BF16)
            + 2 * _nbytes((l_dim, MOBA_BLOCK), F32)
            + 4 * n_pages * _nbytes((PAGE_SIZE, HEAD_TILE, HEAD_DIM), F32) + 8 * _nbytes((s_dim, N_HEADS, HEAD_DIM), F32))
    return pl.pallas_call(
        functools.partial(_moba_kernel, n_blocks=n_blocks, pairs=pairs),
        grid_spec=pltpu.PrefetchScalarGridSpec(
            num_scalar_prefetch=2,
            grid=(b_dim, N_HEADS),
            in_specs=[head_spec, head_spec, head_spec, dec_spec, dec_spec, dec_spec] + page_specs + page_specs,
            out_specs=[head_spec, dec_spec],
            scratch_shapes=[pltpu.VMEM((l_dim, HEAD_DIM), BF16), pltpu.VMEM((HEAD_DIM, l_dim), BF16),
                            pltpu.VMEM((GATE_ROWS, HEAD_DIM), F32), pltpu.VMEM((l_dim, MOBA_BLOCK), BF16)],
        ),
        out_shape=[jax.ShapeDtypeStruct((b_dim, l_dim, hd), BF16),
                   jax.ShapeDtypeStruct((s_dim, N_HEADS, HEAD_DIM), F32)],
        compiler_params=_params(2, vmem),
        name="moba_attention",
    )(page_table_flat, topk_flat, q, k, v, dq, dk_new, dv_new, *([cache_k] * n_pages), *([cache_v] * n_pages))


def _topk_kernel(q_ref, km_ref, o_ref):
    n_blocks = km_ref.shape[0]
    km = km_ref[...].reshape(n_blocks * N_HEADS, HEAD_DIM).astype(BF16)
    gate = lax.dot_general(q_ref[...].astype(BF16), km, (((1,), (1,)), ((), ())), preferred_element_type=F32)
    lane = lax.broadcasted_iota(jnp.int32, gate.shape, 1)
    head = lax.broadcasted_iota(jnp.int32, gate.shape, 0)
    gate = jnp.where((lane & (N_HEADS - 1)) == head, gate, -jnp.inf)
    lane_f = lane.astype(F32)
    out_lane = lax.broadcasted_iota(jnp.int32, o_ref.shape, 1)
    out = jnp.zeros(o_ref.shape, jnp.int32)
    for t in range(MOBA_TOPK):
        best = jnp.max(gate, axis=-1, keepdims=True)
        arg = jnp.min(jnp.where(gate == best, lane_f, float(gate.shape[1])), axis=-1, keepdims=True)
        out = jnp.where(out_lane == t, arg.astype(jnp.int32) // N_HEADS, out)
        gate = jnp.where(lane_f == arg, -jnp.inf, gate)
    o_ref[...] = out


def _topk_call(q, kmean):
    s_dim = q.shape[0]
    n_blocks = kmean.shape[1]
    assert N_HEADS & (N_HEADS - 1) == 0
    vmem = 2 * (_nbytes((N_HEADS, HEAD_DIM), F32) + _nbytes((n_blocks, N_HEADS, HEAD_DIM), F32)
                + _nbytes((N_HEADS, GATE_LANES), jnp.int32)) + 8 * _nbytes((N_HEADS, n_blocks * N_HEADS), F32)
    return pl.pallas_call(
        _topk_kernel,
        grid=(s_dim,),
        in_specs=[pl.BlockSpec((None, N_HEADS, HEAD_DIM), lambda b: (b, 0, 0)),
                  pl.BlockSpec((None, n_blocks, N_HEADS, HEAD_DIM), lambda b: (b, 0, 0, 0))],
        out_specs=pl.BlockSpec((None, N_HEADS, GATE_LANES), lambda b: (b, 0, 0)),
        out_shape=jax.ShapeDtypeStruct((s_dim, N_HEADS, GATE_LANES), jnp.int32),
        compiler_params=_params(1, vmem),
        name="moba_decode_topk",
    )(q, kmean)


FFN_BN = 512


def _trunk(x, mods, kv_mod, fin_mod, weights, *, bm, rows_per_group, gate_fn, q_dtype, v_dtype):
    (g_mix, g_ffn, w_a_in, b_a_in, ln_a_g, ln_a_b, w_a_out, g_kv, w_k, w_v, w_q, w_o,
     w_ff_gate, w_ff_up, w_ff_down, g_fin) = weights
    depth = g_mix.shape[0]
    n_a_layers = w_a_in.shape[0]
    d_a = w_a_in.shape[2] // 2
    half_bm = bm // 2 if bm % 16 == 0 else bm

    def mmr(x, w, layer, epi, **kwargs):
        out = yield "matmul", dict(kwargs, x=x, w=w, layer=layer, epi=epi, bm=kwargs.get("bm", bm))
        return out

    a_rows = []
    k_new = v_new = None
    for l in range(depth):
        if l == n_a_layers:
            kv_norm = (g_kv,) + tuple(kv_mod)
            k_new = yield from mmr(x, w_k, 0, "plain", norm=kv_norm, out_dtype=F32)
            v_new = yield from mmr(x, w_v, 0, "plain", norm=kv_norm, out_dtype=F32)
        sh1, sc1, g1, sh2, sc2, g2 = mods[l]
        mix_norm = (g_mix[l:l + 1], sh1, sc1)
        if l < n_a_layers:
            bias = b_a_in[l:l + 1]
            u = yield from mmr(x, w_a_in, l, "gelu", norm=mix_norm, bias=bias, n_cols=d_a, out_dtype=BF16)
            v_ln = yield from mmr(x, w_a_in, l, "gelu_ln", bm=half_bm, norm=mix_norm, bias=bias,
                                  ln=(ln_a_g[l:l + 1], ln_a_b[l:l + 1]), n_cols=d_a, col_off=d_a, out_dtype=v_dtype)
            a_rows.append(v_ln)
            x = yield from mmr(gate_fn(l, u, v_ln), w_a_out, l, "residual", res=x, gate=g1, out_dtype=F32)
        else:
            j = l - n_a_layers
            q = yield from mmr(x, w_q, j, "plain", norm=mix_norm, out_dtype=q_dtype)
            o = yield "attention", q, k_new, v_new
            x = yield from mmr(o, w_o, j, "residual", res=x, gate=g1, out_dtype=F32)
        h = _norm_call(x, g_ffn[l:l + 1], sh2, sc2, bm=min(bm, 512), rows_per_group=rows_per_group, out_dtype=BF16)
        x = yield "ffn", l, h, x, g2
    f_shift, f_scale = fin_mod
    y = _norm_call(x, g_fin, f_shift, f_scale, bm=min(bm, 512), rows_per_group=rows_per_group, out_dtype=F32)
    return y, k_new, v_new, a_rows


def _send(trunk, value):
    try:
        return trunk.send(value), None
    except StopIteration as done:
        return None, done.value


def kernel(x_prompt, x_sample, c_prompt, c_sample, cache_k, cache_v, page_table, w_ada, b_ada, g_mix, g_ffn, w_a_in, b_a_in, ln_a_g, ln_a_b, w_s, b_s, w_a_out, w_kv_ada, b_kv_ada, g_kv, w_k, w_v, w_q, w_o, w_ff_gate, w_ff_up, w_ff_down, w_fin_ada, b_fin_ada, g_fin):
    batch, seq, d = x_prompt.shape
    n_seq, dec_seq, _ = x_sample.shape
    assert dec_seq == 1 and batch + n_seq <= C_ROWS
    depth = w_ada.shape[0]
    n_pages, page_size, n_heads, head_dim = cache_k.shape
    assert (page_size, n_heads, head_dim) == (PAGE_SIZE, N_HEADS, HEAD_DIM)
    pages_per_seq = page_table.shape[1]
    n_past_blocks = pages_per_seq // PAGES_PER_BLOCK

    c_rows = jnp.concatenate([c_prompt, c_sample, jnp.zeros((C_ROWS - batch - n_seq, d), F32)], axis=0)
    mods = _ada_call(c_rows, w_ada, b_ada[:, None, :])
    kv_mod = _ada_call(c_rows, w_kv_ada[None], b_kv_ada[None, None, :])[0]
    fin_mod = _ada_call(c_rows, w_fin_ada[None], b_fin_ada[None, None, :])[0]

    def prompt_vecs(rows, n):
        return [rows[:batch, i * d:(i + 1) * d][:, None, :] for i in range(n)]

    def sample_vecs(rows, n):
        return [rows[batch:batch + n_seq, i * d:(i + 1) * d][None] for i in range(n)]

    weights = (g_mix, g_ffn, w_a_in, b_a_in, ln_a_g, ln_a_b, w_a_out, g_kv[None], w_k[None], w_v[None], w_q, w_o,
               w_ff_gate, w_ff_up, w_ff_down, g_fin[None])
    b_s_t = jnp.swapaxes(b_s, 1, 2)

    pt_flat = page_table.reshape(-1)
    n_a_layers = w_a_in.shape[0]
    group_w = d // A_GROUPS
    heads = (n_seq, N_HEADS, HEAD_DIM)
    bm_prompt = min(1024, seq)

    ffn_steps = _mmc_steps(batch * seq, w_ff_gate.shape[2], bm_prompt, FFN_BN)
    km_blocks_per_step = -(-(n_seq * n_past_blocks) // (n_a_layers * ffn_steps))

    def prompt_gate(l, u, v_ln):
        return _gmlp_gate_call(u, v_ln, w_s[l], b_s_t[l], rows=512)

    def sample_gate(l, u, v_ln):
        w00 = jnp.repeat(w_s[l, :, 0, 0], group_w)[None, :]
        b0 = jnp.repeat(b_s[l, :, 0], group_w)[None, :]
        return _gmlp_gate_row0_call(u, v_ln, w00, b0)

    prompt = _trunk(
        x_prompt.reshape(batch * seq, d),
        [prompt_vecs(mods[l], 6) for l in range(depth)], prompt_vecs(kv_mod, 2), prompt_vecs(fin_mod, 2),
        weights, bm=bm_prompt, rows_per_group=seq, gate_fn=prompt_gate, q_dtype=BF16, v_dtype=BF16)
    sample = _trunk(
        x_sample.reshape(n_seq, d),
        [sample_vecs(mods[l], 6) for l in range(depth)], sample_vecs(kv_mod, 2), sample_vecs(fin_mod, 2),
        weights, bm=n_seq, rows_per_group=n_seq, gate_fn=sample_gate, q_dtype=F32, v_dtype=F32)

    p_req, s_req = next(prompt), next(sample)
    p_out = s_out = None
    km_parts = []
    kmean = None
    while p_out is None:
        assert p_req[0] == s_req[0]
        if p_req[0] == "matmul":
            main, side = dict(p_req[1]), s_req[1]
            assert main["w"] is side["w"] and (main["layer"], main["epi"]) == (side["layer"], side["epi"])
            to_p, to_s = _mmr_call(main.pop("x"), main.pop("w"), main.pop("layer"), main.pop("epi"), bn=min(1024, d),
                                   rows_per_group=seq, side={key: side.get(key) for key in
                                                             ("x", "norm", "res", "gate", "out_dtype")}, **main)
        elif p_req[0] == "ffn":
            (_, l, h_p, x_p, g2_p), (_, _, h_s, x_s, g2_s) = p_req, s_req
            km_job = None
            if l < n_a_layers:
                km_job = (pt_flat, cache_k, l * ffn_steps * km_blocks_per_step, km_blocks_per_step)
            outs = _mmc_call(h_p, [w_ff_gate, w_ff_up], l, "swiglu", bm=bm_prompt, bn=FFN_BN, side=(h_s,), kmean=km_job)
            a_p, a_s = outs[0], outs[-1]
            if km_job is not None:
                km_parts.append(outs[1])
            to_p, to_s = _mmc_call(a_p, [w_ff_down], l, "residual", bm=bm_prompt // 2, bn=FFN_BN, out_dtype=F32,
                                   res=x_p, gate=g2_p, rows_per_group=seq, side=(a_s, x_s, g2_s))
        else:
            (_, q_p, k_p, v_p), (_, q_s, k_s, v_s) = p_req, s_req
            if kmean is None:
                kmean = jnp.concatenate(km_parts, axis=0)[:n_seq * n_past_blocks]
                kmean = kmean.reshape(n_seq, n_past_blocks, N_HEADS, HEAD_DIM)
            q_s = q_s.reshape(heads)
            topk = _topk_call(q_s, kmean)[:, :, :MOBA_TOPK].reshape(-1)
            o_p, o_s = _moba_call(pt_flat, topk, q_p.reshape(batch, seq, d), k_p.reshape(batch, seq, d),
                                  v_p.reshape(batch, seq, d), q_s, k_s.reshape(heads), v_s.reshape(heads),
                                  cache_k, cache_v, pages_per_seq)
            to_p, to_s = o_p.reshape(batch * seq, d), o_s.reshape(n_seq, d).astype(BF16)
        p_req, p_out = _send(prompt, to_p)
        s_req, s_out = _send(sample, to_s)
    y_p, k_p, v_p, _ = p_out
    y_s, k_s, v_s, a_rows = s_out

    state_a_v = jnp.stack(a_rows, axis=0).reshape(len(a_rows), n_seq, 1, d)
    return (y_p.reshape(batch, seq, d), y_s.reshape(n_seq, 1, d),
            k_p.reshape(batch, seq, N_HEADS, HEAD_DIM), v_p.reshape(batch, seq, N_HEADS, HEAD_DIM),
            k_s.reshape(n_seq, 1, N_HEADS, HEAD_DIM), v_s.reshape(n_seq, 1, N_HEADS, HEAD_DIM),
            state_a_v)
```

```python
import functools

import jax
import jax.numpy as jnp
from jax import lax
from jax.experimental import pallas as pl
from jax.experimental.pallas import tpu as pltpu

F32 = jnp.float32
BF16 = jnp.bfloat16

N_HEADS = 16
HEAD_DIM = 128
CHUNK = 128
A_GROUPS = 16
MOBA_BLOCK = 256
MOBA_TOPK = 3
PAGE_SIZE = 128
PAGES_PER_BLOCK = MOBA_BLOCK // PAGE_SIZE
RMS_EPS = 1e-6
LN_EPS = 1e-5
NEG_INF = -1e30
ATTN_SCALE = HEAD_DIM ** -0.5
GATE_LANES = 128

C_ROWS = 16
MIB = 1024 * 1024
VMEM_LIMIT_V7X = 58 * MIB
COMPILER_TEMP_BYTES = 16 * MIB


def _params(n_axes, vmem_bytes):
    limit = min(int(vmem_bytes) + COMPILER_TEMP_BYTES, VMEM_LIMIT_V7X)
    return pltpu.CompilerParams(dimension_semantics=("arbitrary",) * n_axes,
                                vmem_limit_bytes=limit)


def _nbytes(shape, dtype):
    n = 1
    for s in shape:
        n *= s
    return n * jnp.dtype(dtype).itemsize


def _ada_kernel(c_ref, w_ref, b_ref, o_ref):
    sc = jax.nn.silu(c_ref[...]).astype(BF16)
    o_ref[...] = jnp.dot(sc, w_ref[...].astype(BF16), preferred_element_type=F32) + b_ref[...]


def _ada_call(c_rows, w, b):
    layers, k_dim, n_dim = w.shape
    tn = 1024
    vmem = 2 * (_nbytes((k_dim, tn), F32) + _nbytes((C_ROWS, k_dim), F32) + 2 * _nbytes((C_ROWS, tn), F32))
    return pl.pallas_call(
        _ada_kernel,
        grid=(layers, n_dim // tn),
        in_specs=[pl.BlockSpec((C_ROWS, k_dim), lambda l, n: (0, 0)),
                  pl.BlockSpec((None, k_dim, tn), lambda l, n: (l, 0, n)),
                  pl.BlockSpec((None, 1, tn), lambda l, n: (l, 0, n))],
        out_specs=pl.BlockSpec((None, C_ROWS, tn), lambda l, n: (l, 0, n)),
        out_shape=jax.ShapeDtypeStruct((layers, C_ROWS, n_dim), F32),
        compiler_params=_params(2, vmem),
        name="ada_modulation",
    )(c_rows, w, b)


NORM_CHUNK_ROWS = 16


def _norm_kernel(x_ref, g_ref, sh_ref, sc_ref, o_ref):
    n_rows = x_ref.shape[0]
    chunk = min(NORM_CHUNK_ROWS, n_rows)
    assert sh_ref.shape[0] in (1, chunk)
    gain, shift, scale1 = g_ref[...], sh_ref[...], 1.0 + sc_ref[...]
    for r0 in range(0, n_rows, chunk):
        x = x_ref[r0:r0 + chunk, :]
        y = x * lax.rsqrt(jnp.mean(x * x, axis=-1, keepdims=True) + RMS_EPS) * gain
        o_ref[r0:r0 + chunk, :] = (y * scale1 + shift).astype(o_ref.dtype)


def _norm_call(x, g, shift, scale, bm, rows_per_group, out_dtype):
    m_dim, d = x.shape
    r = shift.shape[1]
    blocks_per_group = rows_per_group // bm
    mod_spec = pl.BlockSpec((None, r, d), lambda m: (m // blocks_per_group, 0, 0))
    vmem = 2 * (_nbytes((bm, d), F32) + _nbytes((bm, d), out_dtype) + 3 * _nbytes((8, d), F32)) + 2 * _nbytes((bm, d), F32)
    return pl.pallas_call(
        _norm_kernel,
        grid=(m_dim // bm,),
        in_specs=[pl.BlockSpec((bm, d), lambda m: (m, 0)),
                  pl.BlockSpec((1, d), lambda m: (0, 0)),
                  mod_spec, mod_spec],
        out_specs=pl.BlockSpec((bm, d), lambda m: (m, 0)),
        out_shape=jax.ShapeDtypeStruct((m_dim, d), out_dtype),
        compiler_params=_params(1, vmem),
        name="rmsnorm_modulate",
    )(x, g, shift, scale)


def _mmc_kernel(*refs, n_w, epi, k_steps, km_blocks, side):
    if km_blocks:
        refs = refs[1:]
    refs = list(refs)

    def take(count):
        taken = refs[:count]
        del refs[:count]
        return taken

    n_vec = 2 if epi == "residual" else 0
    (x_ref,), w_refs, main_vecs = take(1), take(n_w), take(n_vec)
    side_x, side_vecs = (take(1), take(n_vec)) if side else ((), ())
    page_refs = take(km_blocks * PAGES_PER_BLOCK)
    (o_ref,) = take(1)
    km_out = take(1 if km_blocks else 0)
    side_o = take(1 if side else 0)
    wb_refs = take(n_w)
    acc_refs = take(n_w if k_steps > 1 else 0)
    side_acc_refs = take(n_w if side and k_steps > 1 else 0)
    m = pl.program_id(1)
    k = pl.program_id(2)

    def finish(out_ref, vecs, accs):
        if epi == "swiglu":
            out_ref[...] = (jax.nn.silu(accs[0]) * accs[1]).astype(out_ref.dtype)
        elif epi == "residual":
            res_ref, gate_ref = vecs
            out_ref[...] = res_ref[...] + gate_ref[...] * accs[0]

    def row_group(rows_ref, out_ref, vecs, accs_refs):
        rows = rows_ref[...]
        parts = [jnp.dot(rows, wb_ref[k], preferred_element_type=F32) for wb_ref in wb_refs]
        if k_steps == 1:
            finish(out_ref, vecs, parts)
            return

        @pl.when(k == 0)
        def _first():
            for a_ref, p in zip(accs_refs, parts):
                a_ref[...] = p

        if k_steps > 2:
            @pl.when((k > 0) & (k < k_steps - 1))
            def _accumulate():
                for a_ref, p in zip(accs_refs, parts):
                    a_ref[...] += p

        @pl.when(k == k_steps - 1)
        def _last():
            finish(out_ref, vecs, [a_ref[...] + p for a_ref, p in zip(accs_refs, parts)])

    @pl.when(m == 0)
    def _first_row_block():
        for w_ref, wb_ref in zip(w_refs, wb_refs):
            wb_ref[k] = w_ref[...].astype(BF16)
        if side:
            row_group(side_x[0], side_o[0], side_vecs, side_acc_refs)

    row_group(x_ref, o_ref, main_vecs, acc_refs)
    for t in range(km_blocks):
        pages = page_refs[t * PAGES_PER_BLOCK:(t + 1) * PAGES_PER_BLOCK]
        total = functools.reduce(jnp.add, [jnp.sum(p[...], axis=0) for p in pages])
        km_out[0][t] = total * (1.0 / MOBA_BLOCK)


def _mmc_steps(m_dim, n_dim, bm, bn):
    return (n_dim // bn) * (m_dim // bm)


def _mmc_call(x, ws, layer, epi, *, bm, bn, k_steps=1, out_dtype=BF16, res=None, gate=None, rows_per_group=None,
              side=None, kmean=None):
    m_dim, k_dim = x.shape
    n_dim = ws[0].shape[2]
    n_w = len(ws)
    bk = k_dim // k_steps
    m_steps = m_dim // bm
    grid = (n_dim // bn, m_steps, k_steps)
    last_k = k_steps - 1

    def w_index(n, m, k, *_):
        return (layer, jnp.where(m == 0, k, last_k), n)

    in_specs = [pl.BlockSpec((bm, bk), lambda n, m, k, *_: (m, k))]
    in_specs += [pl.BlockSpec((None, bk, bn), w_index) for _ in ws]
    args = [x] + list(ws)
    vmem = 2 * _nbytes((bm, bk), x.dtype) + n_w * 2 * _nbytes((bk, bn), F32)
    if epi == "residual":
        r = gate.shape[1]
        blocks_per_group = rows_per_group // bm
        in_specs.append(pl.BlockSpec((bm, bn), lambda n, m, k, *_: (m, n)))
        in_specs.append(pl.BlockSpec((None, r, bn), lambda n, m, k, *_: (m // blocks_per_group, 0, n)))
        args += [res, gate]
        vmem += 2 * _nbytes((bm, bn), F32) + 2 * _nbytes((max(r, 8), bn), F32)
    if side is not None:
        s_dim = side[0].shape[0]
        in_specs.append(pl.BlockSpec((s_dim, bk), lambda n, m, k, *_: (0, k)))
        if epi == "residual":
            in_specs.append(pl.BlockSpec((s_dim, bn), lambda n, m, k, *_: (0, n)))
            in_specs.append(pl.BlockSpec((None, s_dim, bn), lambda n, m, k, *_: (0, 0, n)))
        args += list(side)
        vmem += 2 * _nbytes((16, bk), BF16) + 8 * _nbytes((16, bn), F32)
    out_specs = [pl.BlockSpec((bm, bn), lambda n, m, k, *_: (m, n))]
    out_shape = [jax.ShapeDtypeStruct((m_dim, n_dim), out_dtype)]
    vmem += 2 * _nbytes((bm, bn), out_dtype)
    km_blocks = 0
    if kmean is not None:
        assert k_steps == 1
        page_table_flat, cache_k, first_block, km_blocks = kmean
        last_block = (page_table_flat.shape[0] // PAGES_PER_BLOCK) - 1
        page_shape = (PAGE_SIZE, N_HEADS, HEAD_DIM)

        def page_spec(t, half):
            def index(n, m, k, pt):
                g = jnp.minimum(first_block + (n * m_steps + m) * km_blocks + t, last_block)
                return (pt[g * PAGES_PER_BLOCK + half], 0, 0, 0)
            return pl.BlockSpec((None,) + page_shape, index)

        in_specs += [page_spec(t, half) for t in range(km_blocks) for half in range(PAGES_PER_BLOCK)]
        args += [cache_k] * (km_blocks * PAGES_PER_BLOCK)
        out_specs.append(pl.BlockSpec((km_blocks, N_HEADS, HEAD_DIM), lambda n, m, k, pt: (n * m_steps + m, 0, 0)))
        out_shape.append(jax.ShapeDtypeStruct((grid[0] * m_steps * km_blocks, N_HEADS, HEAD_DIM), F32))
        vmem += 2 * km_blocks * PAGES_PER_BLOCK * _nbytes(page_shape, F32) + 2 * km_blocks * _nbytes((N_HEADS, HEAD_DIM), F32)
        args = [page_table_flat] + args
    if side is not None:
        out_specs.append(pl.BlockSpec((s_dim, bn), lambda n, m, k, *_: (0, n)))
        out_shape.append(jax.ShapeDtypeStruct((s_dim, n_dim), out_dtype))
    scratch = [pltpu.VMEM((k_steps, bk, bn), BF16) for _ in ws]
    vmem += n_w * _nbytes((k_steps, bk, bn), BF16)
    if k_steps > 1:
        scratch += [pltpu.VMEM((bm, bn), F32) for _ in ws]
        vmem += n_w * _nbytes((bm, bn), F32)
        if side is not None:
            scratch += [pltpu.VMEM((s_dim, bn), F32) for _ in ws]
    outs = pl.pallas_call(
        functools.partial(_mmc_kernel, n_w=n_w, epi=epi, k_steps=k_steps, km_blocks=km_blocks,
                          side=side is not None),
        grid_spec=pltpu.PrefetchScalarGridSpec(
            num_scalar_prefetch=1 if kmean is not None else 0,
            grid=grid,
            in_specs=in_specs,
            out_specs=out_specs,
            scratch_shapes=scratch,
        ),
        out_shape=out_shape,
        compiler_params=_params(3, vmem),
        name="mmc_" + epi,
    )(*args)
    return list(outs)


def _mmr_kernel(*refs, epi, norm, n_steps, side):
    refs = list(refs)

    def take(count):
        taken = refs[:count]
        del refs[:count]
        return taken

    n_mod = 2 if norm else 0
    n_vec = 2 if epi == "residual" else 0
    (x_ref,), g_ref, main_mod = take(1), take(1 if norm else 0), take(n_mod)
    (w_ref,) = take(1)
    bias = take(1 if epi in ("gelu", "gelu_ln") else 0)
    ln = take(2 if epi == "gelu_ln" else 0)
    main_vecs = take(n_vec)
    side_x, side_mod, side_vecs = (take(1), take(n_mod), take(n_vec)) if side else ((), (), ())
    (o_ref,) = take(1)
    side_o = take(1 if side else 0)
    (wb_scr,) = take(1)
    h_scr = take(1 if norm else 0)
    z_scr = take(1 if epi == "gelu_ln" else 0)
    side_h = take(1 if side and norm else 0)
    side_z = take(1 if side and epi == "gelu_ln" else 0)
    m = pl.program_id(0)
    n = pl.program_id(1)

    def row_group(rows_ref, mod, h, out_ref, vecs, z):
        if norm:
            @pl.when(n == 0)
            def _normalize():
                sh_ref, sc_ref = mod
                n_rows = rows_ref.shape[0]
                chunk = min(NORM_CHUNK_ROWS, n_rows)
                assert sh_ref.shape[0] in (1, chunk)
                gain, shift, scale1 = g_ref[0][...], sh_ref[...], 1.0 + sc_ref[...]
                for r0 in range(0, n_rows, chunk):
                    x = rows_ref[r0:r0 + chunk, :]
                    y = x * lax.rsqrt(jnp.mean(x * x, axis=-1, keepdims=True) + RMS_EPS) * gain
                    h[0][r0:r0 + chunk, :] = (y * scale1 + shift).astype(BF16)
            rows = h[0][...]
        else:
            rows = rows_ref[...]
        acc = jnp.dot(rows, wb_scr[n], preferred_element_type=F32)
        if epi == "plain":
            out_ref[...] = acc.astype(out_ref.dtype)
        elif epi == "gelu":
            out_ref[...] = jax.nn.gelu(acc + bias[0][...]).astype(out_ref.dtype)
        elif epi == "residual":
            res_ref, gate_ref = vecs
            out_ref[...] = res_ref[...] + gate_ref[...] * acc
        elif epi == "gelu_ln":
            z[0][n] = jax.nn.gelu(acc + bias[0][...])

            @pl.when(n == n_steps - 1)
            def _layernorm():
                lng_ref, lnb_ref = ln
                bn = z[0].shape[2]
                inv_n = 1.0 / (n_steps * bn)
                zs = [z[0][j] for j in range(n_steps)]
                mu = sum(jnp.sum(zj, axis=-1, keepdims=True) for zj in zs) * inv_n
                var = sum(jnp.sum(jnp.square(zj - mu), axis=-1, keepdims=True) for zj in zs) * inv_n
                rstd = lax.rsqrt(var + LN_EPS)
                for j, zj in enumerate(zs):
                    cols = slice(j * bn, (j + 1) * bn)
                    out_ref[:, cols] = ((zj - mu) * rstd * lng_ref[:, cols] + lnb_ref[:, cols]).astype(out_ref.dtype)

    @pl.when(m == 0)
    def _first_row_block():
        wb_scr[n] = w_ref[...].astype(BF16)
        if side:
            row_group(side_x[0], side_mod, side_h, side_o[0], side_vecs, side_z)

    row_group(x_ref, main_mod, h_scr, o_ref, main_vecs, z_scr)


def _mmr_call(x, w, layer, epi, *, bm, bn, rows_per_group, out_dtype, n_cols=None, col_off=0, norm=None,
              bias=None, ln=None, res=None, gate=None, side=None):
    m_dim, k_dim = x.shape
    n_dim = w.shape[2] if n_cols is None else n_cols
    n_steps = n_dim // bn
    m_steps = m_dim // bm
    off = col_off // bn
    blocks_per_group = rows_per_group // bm

    def first_pass(m, n):
        return jnp.where(m == 0, n, n_steps - 1)

    def w_index(m, n):
        return (layer, 0, off + first_pass(m, n))

    w_buffers = 1 if m_steps > 1 else 2
    in_specs = [pl.BlockSpec((bm, k_dim), lambda m, n: (m, 0))]
    args = [x]
    vmem = 2 * _nbytes((bm, k_dim), x.dtype) + w_buffers * _nbytes((k_dim, bn), F32) + _nbytes((k_dim, n_dim), BF16)
    scratch = [pltpu.VMEM((n_steps, k_dim, bn), BF16)]
    if norm is not None:
        g, shift, scale = norm
        r = shift.shape[1]
        vec_spec = pl.BlockSpec((None, r, k_dim), lambda m, n: (m // blocks_per_group, 0, 0))
        in_specs += [pl.BlockSpec((1, k_dim), lambda m, n: (0, 0)), vec_spec, vec_spec]
        args += [g, shift, scale]
        scratch.append(pltpu.VMEM((bm, k_dim), BF16))
        vmem += 6 * _nbytes((max(r, 8), k_dim), F32) + _nbytes((bm, k_dim), BF16)
    in_specs.append(pl.BlockSpec((None, k_dim, bn), w_index, pipeline_mode=pl.Buffered(w_buffers)))
    args.append(w)
    if epi in ("gelu", "gelu_ln"):
        in_specs.append(pl.BlockSpec((1, bn), lambda m, n: (0, off + n)))
        args.append(bias)
        vmem += 2 * _nbytes((8, bn), F32)
    if epi == "gelu_ln":
        in_specs += [pl.BlockSpec((1, n_dim), lambda m, n: (0, 0))] * 2
        args += list(ln)
        scratch.append(pltpu.VMEM((n_steps, bm, bn), F32))
        vmem += 4 * _nbytes((8, n_dim), F32) + _nbytes((bm, n_dim), F32)
    if epi == "residual":
        r = gate.shape[1]
        in_specs.append(pl.BlockSpec((bm, bn), lambda m, n: (m, n)))
        in_specs.append(pl.BlockSpec((None, r, bn), lambda m, n: (m // blocks_per_group, 0, n)))
        args += [res, gate]
        vmem += 2 * _nbytes((bm, bn), F32) + 2 * _nbytes((max(r, 8), bn), F32)
    if epi == "gelu_ln":
        out_specs = [pl.BlockSpec((bm, n_dim), lambda m, n: (m, 0))]
        vmem += 2 * _nbytes((bm, n_dim), out_dtype)
    else:
        out_specs = [pl.BlockSpec((bm, bn), lambda m, n: (m, n))]
        vmem += 2 * _nbytes((bm, bn), out_dtype)
    out_shape = [jax.ShapeDtypeStruct((m_dim, n_dim), out_dtype)]
    if side is not None:
        s_dim = side["x"].shape[0]
        in_specs.append(pl.BlockSpec((s_dim, k_dim), lambda m, n: (0, 0)))
        args.append(side["x"])
        if norm is not None:
            in_specs += [pl.BlockSpec((None, s_dim, k_dim), lambda m, n: (0, 0, 0))] * 2
            args += list(side["norm"][1:])
            scratch.append(pltpu.VMEM((s_dim, k_dim), BF16))
        if epi == "residual":
            in_specs.append(pl.BlockSpec((s_dim, bn), lambda m, n: (0, first_pass(m, n))))
            in_specs.append(pl.BlockSpec((None, s_dim, bn), lambda m, n: (0, 0, first_pass(m, n))))
            args += [side["res"], side["gate"]]
        if epi == "gelu_ln":
            out_specs.append(pl.BlockSpec((s_dim, n_dim), lambda m, n: (0, 0)))
            scratch.append(pltpu.VMEM((n_steps, s_dim, bn), F32))
        else:
            out_specs.append(pl.BlockSpec((s_dim, bn), lambda m, n: (0, first_pass(m, n))))
        out_shape.append(jax.ShapeDtypeStruct((s_dim, n_dim), side["out_dtype"]))
        vmem += 16 * _nbytes((16, k_dim), F32) + 8 * _nbytes((16, n_dim), F32)
    outs = pl.pallas_call(
        functools.partial(_mmr_kernel, epi=epi, norm=norm is not None, n_steps=n_steps, side=side is not None),
        grid=(m_steps, n_steps),
        in_specs=in_specs,
        out_specs=out_specs,
        out_shape=out_shape,
        scratch_shapes=scratch,
        compiler_params=_params(2, vmem),
        name="mmr_" + epi,
    )(*args)
    return outs[0] if side is None else tuple(outs)


def _gmlp_gate_kernel(u_ref, v_ref, ws_ref, bst_ref, o_ref, wm_scr, *, rows):
    group_w = u_ref.shape[1] // A_GROUPS

    @pl.when(pl.program_id(0) == 0)
    def _mask_weights():
        t = lax.broadcasted_iota(jnp.int32, (CHUNK, CHUNK), 0)
        s = lax.broadcasted_iota(jnp.int32, (CHUNK, CHUNK), 1)
        for g in range(A_GROUPS):
            wm_scr[g] = jnp.where(s <= t, ws_ref[g], 0.0).astype(BF16)

    for c in range(rows // CHUNK):
        chunk = slice(c * CHUNK, (c + 1) * CHUNK)
        for g in range(A_GROUPS):
            cols = slice(g * group_w, (g + 1) * group_w)
            mixed = jnp.dot(wm_scr[g], v_ref[chunk, cols], preferred_element_type=F32) + bst_ref[:, g:g + 1]
            o_ref[chunk, cols] = (u_ref[chunk, cols] * mixed).astype(o_ref.dtype)


def _gmlp_gate_call(u, v, w_s, b_s_t, rows):
    m_dim, d_a = u.shape
    row_spec = pl.BlockSpec((rows, d_a), lambda m: (m, 0))
    vmem = (6 * _nbytes((rows, d_a), BF16) + 3 * _nbytes((A_GROUPS, CHUNK, CHUNK), F32)
            + 2 * _nbytes((CHUNK, 128), F32))
    return pl.pallas_call(
        functools.partial(_gmlp_gate_kernel, rows=rows),
        grid=(m_dim // rows,),
        in_specs=[row_spec, row_spec,
                  pl.BlockSpec((A_GROUPS, CHUNK, CHUNK), lambda m: (0, 0, 0)),
                  pl.BlockSpec((CHUNK, A_GROUPS), lambda m: (0, 0))],
        out_specs=row_spec,
        out_shape=jax.ShapeDtypeStruct((m_dim, d_a), BF16),
        scratch_shapes=[pltpu.VMEM((A_GROUPS, CHUNK, CHUNK), BF16)],
        compiler_params=_params(1, vmem),
        name="gmlp_spatial_gate",
    )(u, v, w_s, b_s_t)


def _gmlp_gate_row0_kernel(u_ref, v_ref, w00_ref, b0_ref, o_ref):
    o_ref[...] = (u_ref[...] * (w00_ref[...] * v_ref[...] + b0_ref[...])).astype(o_ref.dtype)


def _gmlp_gate_row0_call(u, v, w00, b0):
    s_dim, d_a = u.shape
    full = lambda shape: pl.BlockSpec(shape, lambda i: (0,) * len(shape))
    vmem = 2 * (3 * _nbytes((max(s_dim, 16), d_a), F32) + 2 * _nbytes((8, d_a), F32))
    return pl.pallas_call(
        _gmlp_gate_row0_kernel,
        grid=(1,),
        in_specs=[full((s_dim, d_a)), full((s_dim, d_a)), full((1, d_a)), full((1, d_a))],
        out_specs=full((s_dim, d_a)),
        out_shape=jax.ShapeDtypeStruct((s_dim, d_a), BF16),
        compiler_params=_params(1, vmem),
        name="gmlp_gate_single_token",
    )(u, v, w00, b0)


GATE_ROWS = 16
LOG2_E = 1.4426950408889634


HEAD_TILE = 8
N_PICKED_PAGES = MOBA_TOPK * PAGES_PER_BLOCK


def _moba_kernel(pt_ref, tk_ref, q_ref, k_ref, v_ref, dq_ref, dkn_ref, dvn_ref, *refs, n_blocks, pairs):
    del pt_ref, tk_ref
    n_pages = pairs * N_PICKED_PAGES
    dk_refs, dv_refs = refs[:n_pages], refs[n_pages:2 * n_pages]
    o_ref, do_ref, kb_scr, vt_scr, km_scr, p_scr = refs[2 * n_pages:]
    _moba_prefill(q_ref, k_ref, v_ref, o_ref, kb_scr, vt_scr, km_scr, p_scr, n_blocks)
    first = (pl.program_id(0) * N_HEADS + pl.program_id(1)) * pairs
    for t in range(pairs):
        pair = first + t
        seq, head = pair // N_HEADS, pair % N_HEADS
        picked = slice(t * N_PICKED_PAGES, (t + 1) * N_PICKED_PAGES)
        do_ref[seq, pl.ds(head, 1), :] = _decode_attend(
            dq_ref[seq, pl.ds(head, 1), :], dkn_ref[seq, pl.ds(head, 1), :], dvn_ref[seq, pl.ds(head, 1), :],
            dk_refs[picked], dv_refs[picked], head)


def _moba_prefill(q_ref, k_ref, v_ref, o_ref, kb_scr, vt_scr, km_scr, p_scr, n_blocks):
    contract_last = (((1,), (1,)), ((), ()))
    kb_scr[...] = k_ref[...].astype(BF16)
    vt_scr[...] = v_ref[...].T.astype(BF16)
    km_scr[...] = jnp.zeros_like(km_scr)
    for n in range(n_blocks):
        km_scr[n:n + 1, :] = jnp.mean(k_ref[n * MOBA_BLOCK:(n + 1) * MOBA_BLOCK, :], axis=0, keepdims=True)
    gate_all = lax.dot_general(km_scr[...].astype(BF16), q_ref[...], contract_last,
                               preferred_element_type=F32)
    blk = lax.broadcasted_iota(jnp.int32, (GATE_ROWS, MOBA_BLOCK), 0)
    key = lax.broadcasted_iota(jnp.int32, (MOBA_BLOCK, MOBA_BLOCK), 0)
    qry = lax.broadcasted_iota(jnp.int32, (MOBA_BLOCK, MOBA_BLOCK), 1)

    for i in range(n_blocks):
        rows = slice(i * MOBA_BLOCK, (i + 1) * MOBA_BLOCK)
        q_t = q_ref[rows, :]
        bias = [None] * i
        if i > MOBA_TOPK:
            gate = jnp.where(blk < i, gate_all[:, rows], NEG_INF)
            for n in range(i):
                g_n = gate[n:n + 1, :]
                beats = (gate > g_n) | ((gate == g_n) & (blk < n))
                cnt = jnp.sum(jnp.where(beats, 1.0, 0.0), axis=0, keepdims=True)
                bias[n] = jnp.where(cnt < MOBA_TOPK, 0.0, NEG_INF)
        kv_len = (i + 1) * MOBA_BLOCK
        s_all = lax.dot_general(kb_scr[0:kv_len, :], q_t, contract_last,
                                preferred_element_type=F32) * (ATTN_SCALE * LOG2_E)
        s_blocks = []
        for j in range(i + 1):
            s = s_all[j * MOBA_BLOCK:(j + 1) * MOBA_BLOCK, :]
            if j == i:
                s = jnp.where(key <= qry, s, NEG_INF)
            elif bias[j] is not None:
                s = s + bias[j]
            s_blocks.append(s)
        m = jnp.max(functools.reduce(jnp.maximum, s_blocks), axis=0, keepdims=True)
        l = None
        for j, s in enumerate(s_blocks):
            p = jnp.exp2(s - m)
            p_scr[j * MOBA_BLOCK:(j + 1) * MOBA_BLOCK, :] = p.astype(BF16)
            p_sum = jnp.sum(p, axis=0, keepdims=True)
            l = p_sum if l is None else l + p_sum
        acc = jnp.dot(vt_scr[:, 0:kv_len], p_scr[0:kv_len, :], preferred_element_type=F32)
        o_ref[rows, :] = (acc / l).T.astype(o_ref.dtype)


def _decode_attend(q, k_own, v_own, k_refs, v_refs, head):
    s_own = jnp.sum(q * k_own, axis=-1, keepdims=True) * ATTN_SCALE
    q_rows = jnp.broadcast_to(q, (HEAD_TILE, HEAD_DIM)).astype(BF16)
    rows = PAGE_SIZE * HEAD_TILE
    lane = lax.broadcasted_iota(jnp.int32, (HEAD_TILE, rows), 1)
    mine = (lane & (HEAD_TILE - 1)) == (head & (HEAD_TILE - 1))
    scores = []
    for k_ref in k_refs:
        k2 = k_ref[...].reshape(rows, HEAD_DIM).astype(BF16)
        s = lax.dot_general(q_rows, k2, (((1,), (1,)), ((), ())), preferred_element_type=F32) * ATTN_SCALE
        scores.append(jnp.where(mine, s, NEG_INF))
    m = s_own
    for s in scores:
        m = jnp.maximum(m, jnp.max(s, axis=-1, keepdims=True))
    l = jnp.exp(s_own - m)
    acc = l * v_own
    for s, v_ref in zip(scores, v_refs):
        p = jnp.exp(s - m)
        l = l + jnp.sum(p, axis=-1, keepdims=True)
        acc = acc + jnp.dot(p.astype(BF16), v_ref[...].reshape(rows, HEAD_DIM).astype(BF16), preferred_element_type=F32)
    return (acc / l)[0:1, :]


def _moba_call(page_table_flat, topk_flat, q, k, v, dq, dk_new, dv_new, cache_k, cache_v, pages_per_seq):
    b_dim, l_dim, hd = q.shape
    s_dim = dq.shape[0]
    n_blocks = l_dim // MOBA_BLOCK
    assert n_blocks <= GATE_ROWS and s_dim % b_dim == 0 and N_HEADS & (N_HEADS - 1) == 0
    pairs = s_dim // b_dim
    head_spec = pl.BlockSpec((None, l_dim, HEAD_DIM), lambda b, h, pt, tk: (b, 0, h))
    dec_spec = pl.BlockSpec((s_dim, N_HEADS, HEAD_DIM), lambda b, h, pt, tk: (0, 0, 0))

    def page_spec(t, j, half):
        def index(b, h, pt, tk):
            pair = (b * N_HEADS + h) * pairs + t
            seq, head = pair // N_HEADS, pair % N_HEADS
            block = tk[pair * MOBA_TOPK + j]
            return (pt[seq * pages_per_seq + PAGES_PER_BLOCK * block + half], 0, head // HEAD_TILE, 0)
        return pl.BlockSpec((None, PAGE_SIZE, HEAD_TILE, HEAD_DIM), index)

    page_specs = [page_spec(t, j, half) for t in range(pairs) for j in range(MOBA_TOPK)
                  for half in range(PAGES_PER_BLOCK)]
    n_pages = len(page_specs)
    vmem = (4 * _nbytes((l_dim, HEAD_DIM), F32) + 6 * _nbytes((l_dim, HEAD_DIM), BF16)
            + _nbytes((GATE_ROWS, HEAD_DIM), F32) + _nbytes((l_dim, MOBA_BLOCK), BF16)
            + 2 * _nbytes((l_dim, MOBA_BLOCK), F32)
            + 4 * n_pages * _nbytes((PAGE_SIZE, HEAD_TILE, HEAD_DIM), F32) + 8 * _nbytes((s_dim, N_HEADS, HEAD_DIM), F32))
    return pl.pallas_call(
        functools.partial(_moba_kernel, n_blocks=n_blocks, pairs=pairs),
        grid_spec=pltpu.PrefetchScalarGridSpec(
            num_scalar_prefetch=2,
            grid=(b_dim, N_HEADS),
            in_specs=[head_spec, head_spec, head_spec, dec_spec, dec_spec, dec_spec] + page_specs + page_specs,
            out_specs=[head_spec, dec_spec],
            scratch_shapes=[pltpu.VMEM((l_dim, HEAD_DIM), BF16), pltpu.VMEM((HEAD_DIM, l_dim), BF16),
                            pltpu.VMEM((GATE_ROWS, HEAD_DIM), F32), pltpu.VMEM((l_dim, MOBA_BLOCK), BF16)],
        ),
        out_shape=[jax.ShapeDtypeStruct((b_dim, l_dim, hd), BF16),
                   jax.ShapeDtypeStruct((s_dim, N_HEADS, HEAD_DIM), F32)],
        compiler_params=_params(2, vmem),
        name="moba_attention",
    )(page_table_flat, topk_flat, q, k, v, dq, dk_new, dv_new, *([cache_k] * n_pages), *([cache_v] * n_pages))


def _topk_kernel(q_ref, km_ref, o_ref):
    n_blocks = km_ref.shape[0]
    km = km_ref[...].reshape(n_blocks * N_HEADS, HEAD_DIM).astype(BF16)
    gate = lax.dot_general(q_ref[...].astype(BF16), km, (((1,), (1,)), ((), ())), preferred_element_type=F32)
    lane = lax.broadcasted_iota(jnp.int32, gate.shape, 1)
    head = lax.broadcasted_iota(jnp.int32, gate.shape, 0)
    gate = jnp.where((lane & (N_HEADS - 1)) == head, gate, -jnp.inf)
    lane_f = lane.astype(F32)
    out_lane = lax.broadcasted_iota(jnp.int32, o_ref.shape, 1)
    out = jnp.zeros(o_ref.shape, jnp.int32)
    for t in range(MOBA_TOPK):
        best = jnp.max(gate, axis=-1, keepdims=True)
        arg = jnp.min(jnp.where(gate == best, lane_f, float(gate.shape[1])), axis=-1, keepdims=True)
        out = jnp.where(out_lane == t, arg.astype(jnp.int32) // N_HEADS, out)
        gate = jnp.where(lane_f == arg, -jnp.inf, gate)
    o_ref[...] = out


def _topk_call(q, kmean):
    s_dim = q.shape[0]
    n_blocks = kmean.shape[1]
    assert N_HEADS & (N_HEADS - 1) == 0
    vmem = 2 * (_nbytes((N_HEADS, HEAD_DIM), F32) + _nbytes((n_blocks, N_HEADS, HEAD_DIM), F32)
                + _nbytes((N_HEADS, GATE_LANES), jnp.int32)) + 8 * _nbytes((N_HEADS, n_blocks * N_HEADS), F32)
    return pl.pallas_call(
        _topk_kernel,
        grid=(s_dim,),
        in_specs=[pl.BlockSpec((None, N_HEADS, HEAD_DIM), lambda b: (b, 0, 0)),
                  pl.BlockSpec((None, n_blocks, N_HEADS, HEAD_DIM), lambda b: (b, 0, 0, 0))],
        out_specs=pl.BlockSpec((None, N_HEADS, GATE_LANES), lambda b: (b, 0, 0)),
        out_shape=jax.ShapeDtypeStruct((s_dim, N_HEADS, GATE_LANES), jnp.int32),
        compiler_params=_params(1, vmem),
        name="moba_decode_topk",
    )(q, kmean)


FFN_BN = 512


def _trunk(x, mods, kv_mod, fin_mod, weights, *, bm, rows_per_group, gate_fn, q_dtype, v_dtype):
    (g_mix, g_ffn, w_a_in, b_a_in, ln_a_g, ln_a_b, w_a_out, g_kv, w_k, w_v, w_q, w_o,
     w_ff_gate, w_ff_up, w_ff_down, g_fin) = weights
    depth = g_mix.shape[0]
    n_a_layers = w_a_in.shape[0]
    d_a = w_a_in.shape[2] // 2
    half_bm = bm // 2 if bm % 16 == 0 else bm

    def mmr(x, w, layer, epi, **kwargs):
        out = yield "matmul", dict(kwargs, x=x, w=w, layer=layer, epi=epi, bm=kwargs.get("bm", bm))
        return out

    a_rows = []
    k_new = v_new = None
    for l in range(depth):
        if l == n_a_layers:
            kv_norm = (g_kv,) + tuple(kv_mod)
            k_new = yield from mmr(x, w_k, 0, "plain", norm=kv_norm, out_dtype=F32)
            v_new = yield from mmr(x, w_v, 0, "plain", norm=kv_norm, out_dtype=F32)
        sh1, sc1, g1, sh2, sc2, g2 = mods[l]
        mix_norm = (g_mix[l:l + 1], sh1, sc1)
        if l < n_a_layers:
            bias = b_a_in[l:l + 1]
            u = yield from mmr(x, w_a_in, l, "gelu", norm=mix_norm, bias=bias, n_cols=d_a, out_dtype=BF16)
            v_ln = yield from mmr(x, w_a_in, l, "gelu_ln", bm=half_bm, norm=mix_norm, bias=bias,
                                  ln=(ln_a_g[l:l + 1], ln_a_b[l:l + 1]), n_cols=d_a, col_off=d_a, out_dtype=v_dtype)
            a_rows.append(v_ln)
            x = yield from mmr(gate_fn(l, u, v_ln), w_a_out, l, "residual", res=x, gate=g1, out_dtype=F32)
        else:
            j = l - n_a_layers
            q = yield from mmr(x, w_q, j, "plain", norm=mix_norm, out_dtype=q_dtype)
            o = yield "attention", q, k_new, v_new
            x = yield from mmr(o, w_o, j, "residual", res=x, gate=g1, out_dtype=F32)
        h = _norm_call(x, g_ffn[l:l + 1], sh2, sc2, bm=min(bm, 512), rows_per_group=rows_per_group, out_dtype=BF16)
        x = yield "ffn", l, h, x, g2
    f_shift, f_scale = fin_mod
    y = _norm_call(x, g_fin, f_shift, f_scale, bm=min(bm, 512), rows_per_group=rows_per_group, out_dtype=F32)
    return y, k_new, v_new, a_rows


def _send(trunk, value):
    try:
        return trunk.send(value), None
    except StopIteration as done:
        return None, done.value


def kernel(x_prompt, x_sample, c_prompt, c_sample, cache_k, cache_v, page_table, w_ada, b_ada, g_mix, g_ffn, w_a_in, b_a_in, ln_a_g, ln_a_b, w_s, b_s, w_a_out, w_kv_ada, b_kv_ada, g_kv, w_k, w_v, w_q, w_o, w_ff_gate, w_ff_up, w_ff_down, w_fin_ada, b_fin_ada, g_fin):
    batch, seq, d = x_prompt.shape
    n_seq, dec_seq, _ = x_sample.shape
    assert dec_seq == 1 and batch + n_seq <= C_ROWS
    depth = w_ada.shape[0]
    n_pages, page_size, n_heads, head_dim = cache_k.shape
    assert (page_size, n_heads, head_dim) == (PAGE_SIZE, N_HEADS, HEAD_DIM)
    pages_per_seq = page_table.shape[1]
    n_past_blocks = pages_per_seq // PAGES_PER_BLOCK

    c_rows = jnp.concatenate([c_prompt, c_sample, jnp.zeros((C_ROWS - batch - n_seq, d), F32)], axis=0)
    mods = _ada_call(c_rows, w_ada, b_ada[:, None, :])
    kv_mod = _ada_call(c_rows, w_kv_ada[None], b_kv_ada[None, None, :])[0]
    fin_mod = _ada_call(c_rows, w_fin_ada[None], b_fin_ada[None, None, :])[0]

    def prompt_vecs(rows, n):
        return [rows[:batch, i * d:(i + 1) * d][:, None, :] for i in range(n)]

    def sample_vecs(rows, n):
        return [rows[batch:batch + n_seq, i * d:(i + 1) * d][None] for i in range(n)]

    weights = (g_mix, g_ffn, w_a_in, b_a_in, ln_a_g, ln_a_b, w_a_out, g_kv[None], w_k[None], w_v[None], w_q, w_o,
               w_ff_gate, w_ff_up, w_ff_down, g_fin[None])
    b_s_t = jnp.swapaxes(b_s, 1, 2)

    pt_flat = page_table.reshape(-1)
    n_a_layers = w_a_in.shape[0]
    group_w = d // A_GROUPS
    heads = (n_seq, N_HEADS, HEAD_DIM)
    bm_prompt = min(1024, seq)

    ffn_steps = _mmc_steps(batch * seq, w_ff_gate.shape[2], bm_prompt, FFN_BN)
    km_blocks_per_step = -(-(n_seq * n_past_blocks) // (n_a_layers * ffn_steps))

    def prompt_gate(l, u, v_ln):
        return _gmlp_gate_call(u, v_ln, w_s[l], b_s_t[l], rows=512)

    def sample_gate(l, u, v_ln):
        w00 = jnp.repeat(w_s[l, :, 0, 0], group_w)[None, :]
        b0 = jnp.repeat(b_s[l, :, 0], group_w)[None, :]
        return _gmlp_gate_row0_call(u, v_ln, w00, b0)

    prompt = _trunk(
        x_prompt.reshape(batch * seq, d),
        [prompt_vecs(mods[l], 6) for l in range(depth)], prompt_vecs(kv_mod, 2), prompt_vecs(fin_mod, 2),
        weights, bm=bm_prompt, rows_per_group=seq, gate_fn=prompt_gate, q_dtype=BF16, v_dtype=BF16)
    sample = _trunk(
        x_sample.reshape(n_seq, d),
        [sample_vecs(mods[l], 6) for l in range(depth)], sample_vecs(kv_mod, 2), sample_vecs(fin_mod, 2),
        weights, bm=n_seq, rows_per_group=n_seq, gate_fn=sample_gate, q_dtype=F32, v_dtype=F32)

    p_req, s_req = next(prompt), next(sample)
    p_out = s_out = None
    km_parts = []
    kmean = None
    while p_out is None:
        assert p_req[0] == s_req[0]
        if p_req[0] == "matmul":
            main, side = dict(p_req[1]), s_req[1]
            assert main["w"] is side["w"] and (main["layer"], main["epi"]) == (side["layer"], side["epi"])
            to_p, to_s = _mmr_call(main.pop("x"), main.pop("w"), main.pop("layer"), main.pop("epi"), bn=min(1024, d),
                                   rows_per_group=seq, side={key: side.get(key) for key in
                                                             ("x", "norm", "res", "gate", "out_dtype")}, **main)
        elif p_req[0] == "ffn":
            (_, l, h_p, x_p, g2_p), (_, _, h_s, x_s, g2_s) = p_req, s_req
            km_job = None
            if l < n_a_layers:
                km_job = (pt_flat, cache_k, l * ffn_steps * km_blocks_per_step, km_blocks_per_step)
            outs = _mmc_call(h_p, [w_ff_gate, w_ff_up], l, "swiglu", bm=bm_prompt, bn=FFN_BN, side=(h_s,), kmean=km_job)
            a_p, a_s = outs[0], outs[-1]
            if km_job is not None:
                km_parts.append(outs[1])
            to_p, to_s = _mmc_call(a_p, [w_ff_down], l, "residual", bm=bm_prompt // 2, bn=FFN_BN, out_dtype=F32,
                                   res=x_p, gate=g2_p, rows_per_group=seq, side=(a_s, x_s, g2_s))
        else:
            (_, q_p, k_p, v_p), (_, q_s, k_s, v_s) = p_req, s_req
            if kmean is None:
                kmean = jnp.concatenate(km_parts, axis=0)[:n_seq * n_past_blocks]
                kmean = kmean.reshape(n_seq, n_past_blocks, N_HEADS, HEAD_DIM)
            q_s = q_s.reshape(heads)
            topk = _topk_call(q_s, kmean)[:, :, :MOBA_TOPK].reshape(-1)
            o_p, o_s = _moba_call(pt_flat, topk, q_p.reshape(batch, seq, d), k_p.reshape(batch, seq, d),
                                  v_p.reshape(batch, seq, d), q_s, k_s.reshape(heads), v_s.reshape(heads),
                                  cache_k, cache_v, pages_per_seq)
            to_p, to_s = o_p.reshape(batch * seq, d), o_s.reshape(n_seq, d).astype(BF16)
        p_req, p_out = _send(prompt, to_p)
        s_req, s_out = _send(sample, to_s)
    y_p, k_p, v_p, _ = p_out
    y_s, k_s, v_s, a_rows = s_out

    state_a_v = jnp.stack(a_rows, axis=0).reshape(len(a_rows), n_seq, 1, d)
    return (y_p.reshape(batch, seq, d), y_s.reshape(n_seq, 1, d),
            k_p.reshape(batch, seq, N_HEADS, HEAD_DIM), v_p.reshape(batch, seq, N_HEADS, HEAD_DIM),
            k_s.reshape(n_seq, 1, N_HEADS, HEAD_DIM), v_s.reshape(n_seq, 1, N_HEADS, HEAD_DIM),
            state_a_v)
```

```python
import functools

import jax
import jax.numpy as jnp
from jax import lax
from jax.experimental import pallas as pl
from jax.experimental.pallas import tpu as pltpu

F32 = jnp.float32
BF16 = jnp.bfloat16

N_HEADS = 16
HEAD_DIM = 128
CHUNK = 128
A_GROUPS = 16
MOBA_BLOCK = 256
MOBA_TOPK = 3
PAGE_SIZE = 128
PAGES_PER_BLOCK = MOBA_BLOCK // PAGE_SIZE
RMS_EPS = 1e-6
LN_EPS = 1e-5
NEG_INF = -1e30
ATTN_SCALE = HEAD_DIM ** -0.5
GATE_LANES = 128

C_ROWS = 16
MIB = 1024 * 1024
VMEM_LIMIT_V7X = 58 * MIB
COMPILER_TEMP_BYTES = 16 * MIB


def _params(n_axes, vmem_bytes):
    limit = min(int(vmem_bytes) + COMPILER_TEMP_BYTES, VMEM_LIMIT_V7X)
    return pltpu.CompilerParams(dimension_semantics=("arbitrary",) * n_axes,
                                vmem_limit_bytes=limit)


def _nbytes(shape, dtype):
    n = 1
    for s in shape:
        n *= s
    return n * jnp.dtype(dtype).itemsize


def _ada_kernel(c_ref, w_ref, b_ref, o_ref):
    sc = jax.nn.silu(c_ref[...]).astype(BF16)
    o_ref[...] = jnp.dot(sc, w_ref[...].astype(BF16), preferred_element_type=F32) + b_ref[...]


def _ada_call(c_rows, w, b):
    layers, k_dim, n_dim = w.shape
    tn = 1024
    vmem = 2 * (_nbytes((k_dim, tn), F32) + _nbytes((C_ROWS, k_dim), F32) + 2 * _nbytes((C_ROWS, tn), F32))
    return pl.pallas_call(
        _ada_kernel,
        grid=(layers, n_dim // tn),
        in_specs=[pl.BlockSpec((C_ROWS, k_dim), lambda l, n: (0, 0)),
                  pl.BlockSpec((None, k_dim, tn), lambda l, n: (l, 0, n)),
                  pl.BlockSpec((None, 1, tn), lambda l, n: (l, 0, n))],
        out_specs=pl.BlockSpec((None, C_ROWS, tn), lambda l, n: (l, 0, n)),
        out_shape=jax.ShapeDtypeStruct((layers, C_ROWS, n_dim), F32),
        compiler_params=_params(2, vmem),
        name="ada_modulation",
    )(c_rows, w, b)


NORM_CHUNK_ROWS = 16


def _norm_kernel(x_ref, g_ref, sh_ref, sc_ref, o_ref):
    n_rows = x_ref.shape[0]
    chunk = min(NORM_CHUNK_ROWS, n_rows)
    assert sh_ref.shape[0] in (1, chunk)
    gain, shift, scale1 = g_ref[...], sh_ref[...], 1.0 + sc_ref[...]
    for r0 in range(0, n_rows, chunk):
        x = x_ref[r0:r0 + chunk, :]
        y = x * lax.rsqrt(jnp.mean(x * x, axis=-1, keepdims=True) + RMS_EPS) * gain
        o_ref[r0:r0 + chunk, :] = (y * scale1 + shift).astype(o_ref.dtype)


def _norm_call(x, g, shift, scale, bm, rows_per_group, out_dtype):
    m_dim, d = x.shape
    r = shift.shape[1]
    blocks_per_group = rows_per_group // bm
    mod_spec = pl.BlockSpec((None, r, d), lambda m: (m // blocks_per_group, 0, 0))
    vmem = 2 * (_nbytes((bm, d), F32) + _nbytes((bm, d), out_dtype) + 3 * _nbytes((8, d), F32)) + 2 * _nbytes((bm, d), F32)
    return pl.pallas_call(
        _norm_kernel,
        grid=(m_dim // bm,),
        in_specs=[pl.BlockSpec((bm, d), lambda m: (m, 0)),
                  pl.BlockSpec((1, d), lambda m: (0, 0)),
                  mod_spec, mod_spec],
        out_specs=pl.BlockSpec((bm, d), lambda m: (m, 0)),
        out_shape=jax.ShapeDtypeStruct((m_dim, d), out_dtype),
        compiler_params=_params(1, vmem),
        name="rmsnorm_modulate",
    )(x, g, shift, scale)


def _mmc_kernel(*refs, n_w, epi, k_steps, km_blocks, side):
    if km_blocks:
        refs = refs[1:]
    refs = list(refs)

    def take(count):
        taken = refs[:count]
        del refs[:count]
        return taken

    n_vec = 2 if epi == "residual" else 0
    (x_ref,), w_refs, main_vecs = take(1), take(n_w), take(n_vec)
    side_x, side_vecs = (take(1), take(n_vec)) if side else ((), ())
    page_refs = take(km_blocks * PAGES_PER_BLOCK)
    (o_ref,) = take(1)
    km_out = take(1 if km_blocks else 0)
    side_o = take(1 if side else 0)
    wb_refs = take(n_w)
    acc_refs = take(n_w if k_steps > 1 else 0)
    side_acc_refs = take(n_w if side and k_steps > 1 else 0)
    m = pl.program_id(1)
    k = pl.program_id(2)

    def finish(out_ref, vecs, accs):
        if epi == "swiglu":
            out_ref[...] = (jax.nn.silu(accs[0]) * accs[1]).astype(out_ref.dtype)
        elif epi == "residual":
            res_ref, gate_ref = vecs
            out_ref[...] = res_ref[...] + gate_ref[...] * accs[0]

    def row_group(rows_ref, out_ref, vecs, accs_refs):
        rows = rows_ref[...]
        parts = [jnp.dot(rows, wb_ref[k], preferred_element_type=F32) for wb_ref in wb_refs]
        if k_steps == 1:
            finish(out_ref, vecs, parts)
            return

        @pl.when(k == 0)
        def _first():
            for a_ref, p in zip(accs_refs, parts):
                a_ref[...] = p

        if k_steps > 2:
            @pl.when((k > 0) & (k < k_steps - 1))
            def _accumulate():
                for a_ref, p in zip(accs_refs, parts):
                    a_ref[...] += p

        @pl.when(k == k_steps - 1)
        def _last():
            finish(out_ref, vecs, [a_ref[...] + p for a_ref, p in zip(accs_refs, parts)])

    @pl.when(m == 0)
    def _first_row_block():
        for w_ref, wb_ref in zip(w_refs, wb_refs):
            wb_ref[k] = w_ref[...].astype(BF16)
        if side:
            row_group(side_x[0], side_o[0], side_vecs, side_acc_refs)

    row_group(x_ref, o_ref, main_vecs, acc_refs)
    for t in range(km_blocks):
        pages = page_refs[t * PAGES_PER_BLOCK:(t + 1) * PAGES_PER_BLOCK]
        total = functools.reduce(jnp.add, [jnp.sum(p[...], axis=0) for p in pages])
        km_out[0][t] = total * (1.0 / MOBA_BLOCK)


def _mmc_steps(m_dim, n_dim, bm, bn):
    return (n_dim // bn) * (m_dim // bm)


def _mmc_call(x, ws, layer, epi, *, bm, bn, k_steps=1, out_dtype=BF16, res=None, gate=None, rows_per_group=None,
              side=None, kmean=None):
    m_dim, k_dim = x.shape
    n_dim = ws[0].shape[2]
    n_w = len(ws)
    bk = k_dim // k_steps
    m_steps = m_dim // bm
    grid = (n_dim // bn, m_steps, k_steps)
    last_k = k_steps - 1

    def w_index(n, m, k, *_):
        return (layer, jnp.where(m == 0, k, last_k), n)

    in_specs = [pl.BlockSpec((bm, bk), lambda n, m, k, *_: (m, k))]
    in_specs += [pl.BlockSpec((None, bk, bn), w_index) for _ in ws]
    args = [x] + list(ws)
    vmem = 2 * _nbytes((bm, bk), x.dtype) + n_w * 2 * _nbytes((bk, bn), F32)
    if epi == "residual":
        r = gate.shape[1]
        blocks_per_group = rows_per_group // bm
        in_specs.append(pl.BlockSpec((bm, bn), lambda n, m, k, *_: (m, n)))
        in_specs.append(pl.BlockSpec((None, r, bn), lambda n, m, k, *_: (m // blocks_per_group, 0, n)))
        args += [res, gate]
        vmem += 2 * _nbytes((bm, bn), F32) + 2 * _nbytes((max(r, 8), bn), F32)
    if side is not None:
        s_dim = side[0].shape[0]
        in_specs.append(pl.BlockSpec((s_dim, bk), lambda n, m, k, *_: (0, k)))
        if epi == "residual":
            in_specs.append(pl.BlockSpec((s_dim, bn), lambda n, m, k, *_: (0, n)))
            in_specs.append(pl.BlockSpec((None, s_dim, bn), lambda n, m, k, *_: (0, 0, n)))
        args += list(side)
        vmem += 2 * _nbytes((16, bk), BF16) + 8 * _nbytes((16, bn), F32)
    out_specs = [pl.BlockSpec((bm, bn), lambda n, m, k, *_: (m, n))]
    out_shape = [jax.ShapeDtypeStruct((m_dim, n_dim), out_dtype)]
    vmem += 2 * _nbytes((bm, bn), out_dtype)
    km_blocks = 0
    if kmean is not None:
        assert k_steps == 1
        page_table_flat, cache_k, first_block, km_blocks = kmean
        last_block = (page_table_flat.shape[0] // PAGES_PER_BLOCK) - 1
        page_shape = (PAGE_SIZE, N_HEADS, HEAD_DIM)

        def page_spec(t, half):
            def index(n, m, k, pt):
                g = jnp.minimum(first_block + (n * m_steps + m) * km_blocks + t, last_block)
                return (pt[g * PAGES_PER_BLOCK + half], 0, 0, 0)
            return pl.BlockSpec((None,) + page_shape, index)

        in_specs += [page_spec(t, half) for t in range(km_blocks) for half in range(PAGES_PER_BLOCK)]
        args += [cache_k] * (km_blocks * PAGES_PER_BLOCK)
        out_specs.append(pl.BlockSpec((km_blocks, N_HEADS, HEAD_DIM), lambda n, m, k, pt: (n * m_steps + m, 0, 0)))
        out_shape.append(jax.ShapeDtypeStruct((grid[0] * m_steps * km_blocks, N_HEADS, HEAD_DIM), F32))
        vmem += 2 * km_blocks * PAGES_PER_BLOCK * _nbytes(page_shape, F32) + 2 * km_blocks * _nbytes((N_HEADS, HEAD_DIM), F32)
        args = [page_table_flat] + args
    if side is not None:
        out_specs.append(pl.BlockSpec((s_dim, bn), lambda n, m, k, *_: (0, n)))
        out_shape.append(jax.ShapeDtypeStruct((s_dim, n_dim), out_dtype))
    scratch = [pltpu.VMEM((k_steps, bk, bn), BF16) for _ in ws]
    vmem += n_w * _nbytes((k_steps, bk, bn), BF16)
    if k_steps > 1:
        scratch += [pltpu.VMEM((bm, bn), F32) for _ in ws]
        vmem += n_w * _nbytes((bm, bn), F32)
        if side is not None:
            scratch += [pltpu.VMEM((s_dim, bn), F32) for _ in ws]
    outs = pl.pallas_call(
        functools.partial(_mmc_kernel, n_w=n_w, epi=epi, k_steps=k_steps, km_blocks=km_blocks,
                          side=side is not None),
        grid_spec=pltpu.PrefetchScalarGridSpec(
            num_scalar_prefetch=1 if kmean is not None else 0,
            grid=grid,
            in_specs=in_specs,
            out_specs=out_specs,
            scratch_shapes=scratch,
        ),
        out_shape=out_shape,
        compiler_params=_params(3, vmem),
        name="mmc_" + epi,
    )(*args)
    return list(outs)


def _mmr_kernel(*refs, epi, norm, n_steps, side, emit_h):
    refs = list(refs)

    def take(count):
        taken = refs[:count]
        del refs[:count]
        return taken

    n_mod = 2 if norm else 0
    n_vec = 2 if epi == "residual" else 0
    (x_ref,), g_ref, main_mod = take(1), take(1 if norm else 0), take(n_mod)
    (w_ref,) = take(1)
    bias = take(1 if epi in ("gelu", "gelu_ln") else 0)
    ln = take(2 if epi == "gelu_ln" else 0)
    main_vecs = take(n_vec)
    side_x, side_mod, side_vecs = (take(1), take(n_mod), take(n_vec)) if side else ((), (), ())
    (o_ref,) = take(1)
    h_out = take(1 if emit_h else 0)
    side_o = take(1 if side else 0)
    side_h_out = take(1 if side and emit_h else 0)
    (wb_scr,) = take(1)
    h_scr = h_out if emit_h else take(1 if norm else 0)
    z_scr = take(1 if epi == "gelu_ln" else 0)
    side_h = side_h_out if emit_h else take(1 if side and norm else 0)
    side_z = take(1 if side and epi == "gelu_ln" else 0)
    m = pl.program_id(0)
    n = pl.program_id(1)

    def row_group(rows_ref, mod, h, out_ref, vecs, z):
        if norm:
            @pl.when(n == 0)
            def _normalize():
                sh_ref, sc_ref = mod
                n_rows = rows_ref.shape[0]
                chunk = min(NORM_CHUNK_ROWS, n_rows)
                assert sh_ref.shape[0] in (1, chunk)
                gain, shift, scale1 = g_ref[0][...], sh_ref[...], 1.0 + sc_ref[...]
                for r0 in range(0, n_rows, chunk):
                    x = rows_ref[r0:r0 + chunk, :]
                    y = x * lax.rsqrt(jnp.mean(x * x, axis=-1, keepdims=True) + RMS_EPS) * gain
                    h[0][r0:r0 + chunk, :] = (y * scale1 + shift).astype(BF16)
            rows = h[0][...]
        else:
            rows = rows_ref[...]
        acc = jnp.dot(rows, wb_scr[n], preferred_element_type=F32)
        if epi == "plain":
            out_ref[...] = acc.astype(out_ref.dtype)
        elif epi == "gelu":
            out_ref[...] = jax.nn.gelu(acc + bias[0][...]).astype(out_ref.dtype)
        elif epi == "residual":
            res_ref, gate_ref = vecs
            out_ref[...] = res_ref[...] + gate_ref[...] * acc
        elif epi == "gelu_ln":
            z[0][n] = jax.nn.gelu(acc + bias[0][...])

            @pl.when(n == n_steps - 1)
            def _layernorm():
                lng_ref, lnb_ref = ln
                bn = z[0].shape[2]
                inv_n = 1.0 / (n_steps * bn)
                zs = [z[0][j] for j in range(n_steps)]
                mu = sum(jnp.sum(zj, axis=-1, keepdims=True) for zj in zs) * inv_n
                var = sum(jnp.sum(jnp.square(zj - mu), axis=-1, keepdims=True) for zj in zs) * inv_n
                rstd = lax.rsqrt(var + LN_EPS)
                for j, zj in enumerate(zs):
                    cols = slice(j * bn, (j + 1) * bn)
                    out_ref[:, cols] = ((zj - mu) * rstd * lng_ref[:, cols] + lnb_ref[:, cols]).astype(out_ref.dtype)

    @pl.when(m == 0)
    def _first_row_block():
        wb_scr[n] = w_ref[...].astype(BF16)
        if side:
            row_group(side_x[0], side_mod, side_h, side_o[0], side_vecs, side_z)

    row_group(x_ref, main_mod, h_scr, o_ref, main_vecs, z_scr)


def _mmr_call(x, w, layer, epi, *, bm, bn, rows_per_group, out_dtype, n_cols=None, col_off=0, norm=None,
              bias=None, ln=None, res=None, gate=None, side=None, emit_h=False):
    m_dim, k_dim = x.shape
    n_dim = w.shape[2] if n_cols is None else n_cols
    n_steps = n_dim // bn
    m_steps = m_dim // bm
    off = col_off // bn
    blocks_per_group = rows_per_group // bm

    def first_pass(m, n):
        return jnp.where(m == 0, n, n_steps - 1)

    def w_index(m, n):
        return (layer, 0, off + first_pass(m, n))

    w_buffers = 1 if m_steps > 1 else 2
    in_specs = [pl.BlockSpec((bm, k_dim), lambda m, n: (m, 0))]
    args = [x]
    vmem = 2 * _nbytes((bm, k_dim), x.dtype) + w_buffers * _nbytes((k_dim, bn), F32) + _nbytes((k_dim, n_dim), BF16)
    scratch = [pltpu.VMEM((n_steps, k_dim, bn), BF16)]
    if norm is not None:
        g, shift, scale = norm
        r = shift.shape[1]
        vec_spec = pl.BlockSpec((None, r, k_dim), lambda m, n: (m // blocks_per_group, 0, 0))
        in_specs += [pl.BlockSpec((1, k_dim), lambda m, n: (0, 0)), vec_spec, vec_spec]
        args += [g, shift, scale]
        if not emit_h:
            scratch.append(pltpu.VMEM((bm, k_dim), BF16))
        vmem += 6 * _nbytes((max(r, 8), k_dim), F32) + 2 * _nbytes((bm, k_dim), BF16)
    in_specs.append(pl.BlockSpec((None, k_dim, bn), w_index, pipeline_mode=pl.Buffered(w_buffers)))
    args.append(w)
    if epi in ("gelu", "gelu_ln"):
        in_specs.append(pl.BlockSpec((1, bn), lambda m, n: (0, off + n)))
        args.append(bias)
        vmem += 2 * _nbytes((8, bn), F32)
    if epi == "gelu_ln":
        in_specs += [pl.BlockSpec((1, n_dim), lambda m, n: (0, 0))] * 2
        args += list(ln)
        scratch.append(pltpu.VMEM((n_steps, bm, bn), F32))
        vmem += 4 * _nbytes((8, n_dim), F32) + _nbytes((bm, n_dim), F32)
    if epi == "residual":
        r = gate.shape[1]
        in_specs.append(pl.BlockSpec((bm, bn), lambda m, n: (m, n)))
        in_specs.append(pl.BlockSpec((None, r, bn), lambda m, n: (m // blocks_per_group, 0, n)))
        args += [res, gate]
        vmem += 2 * _nbytes((bm, bn), F32) + 2 * _nbytes((max(r, 8), bn), F32)
    if epi == "gelu_ln":
        out_specs = [pl.BlockSpec((bm, n_dim), lambda m, n: (m, 0))]
        vmem += 2 * _nbytes((bm, n_dim), out_dtype)
    else:
        out_specs = [pl.BlockSpec((bm, bn), lambda m, n: (m, n))]
        vmem += 2 * _nbytes((bm, bn), out_dtype)
    out_shape = [jax.ShapeDtypeStruct((m_dim, n_dim), out_dtype)]
    assert not emit_h or (norm is not None and side is not None)
    if emit_h:
        out_specs.append(pl.BlockSpec((bm, k_dim), lambda m, n: (m, 0)))
        out_shape.append(jax.ShapeDtypeStruct((m_dim, k_dim), BF16))
    if side is not None:
        s_dim = side["x"].shape[0]
        in_specs.append(pl.BlockSpec((s_dim, k_dim), lambda m, n: (0, 0)))
        args.append(side["x"])
        if norm is not None:
            in_specs += [pl.BlockSpec((None, s_dim, k_dim), lambda m, n: (0, 0, 0))] * 2
            args += list(side["norm"][1:])
            if not emit_h:
                scratch.append(pltpu.VMEM((s_dim, k_dim), BF16))
        if epi == "residual":
            in_specs.append(pl.BlockSpec((s_dim, bn), lambda m, n: (0, first_pass(m, n))))
            in_specs.append(pl.BlockSpec((None, s_dim, bn), lambda m, n: (0, 0, first_pass(m, n))))
            args += [side["res"], side["gate"]]
        if epi == "gelu_ln":
            out_specs.append(pl.BlockSpec((s_dim, n_dim), lambda m, n: (0, 0)))
            scratch.append(pltpu.VMEM((n_steps, s_dim, bn), F32))
        else:
            out_specs.append(pl.BlockSpec((s_dim, bn), lambda m, n: (0, first_pass(m, n))))
        out_shape.append(jax.ShapeDtypeStruct((s_dim, n_dim), side["out_dtype"]))
        if emit_h:
            out_specs.append(pl.BlockSpec((s_dim, k_dim), lambda m, n: (0, 0)))
            out_shape.append(jax.ShapeDtypeStruct((s_dim, k_dim), BF16))
        vmem += 16 * _nbytes((16, k_dim), F32) + 8 * _nbytes((16, n_dim), F32)
    outs = pl.pallas_call(
        functools.partial(_mmr_kernel, epi=epi, norm=norm is not None, n_steps=n_steps, side=side is not None,
                          emit_h=emit_h),
        grid=(m_steps, n_steps),
        in_specs=in_specs,
        out_specs=out_specs,
        out_shape=out_shape,
        scratch_shapes=scratch,
        compiler_params=_params(2, vmem),
        name="mmr_" + epi,
    )(*args)
    return outs[0] if side is None else tuple(outs)


def _gmlp_gate_kernel(u_ref, v_ref, ws_ref, bst_ref, o_ref, wm_scr, *, rows):
    group_w = u_ref.shape[1] // A_GROUPS

    @pl.when(pl.program_id(0) == 0)
    def _mask_weights():
        t = lax.broadcasted_iota(jnp.int32, (CHUNK, CHUNK), 0)
        s = lax.broadcasted_iota(jnp.int32, (CHUNK, CHUNK), 1)
        for g in range(A_GROUPS):
            wm_scr[g] = jnp.where(s <= t, ws_ref[g], 0.0).astype(BF16)

    for c in range(rows // CHUNK):
        chunk = slice(c * CHUNK, (c + 1) * CHUNK)
        for g in range(A_GROUPS):
            cols = slice(g * group_w, (g + 1) * group_w)
            mixed = jnp.dot(wm_scr[g], v_ref[chunk, cols], preferred_element_type=F32) + bst_ref[:, g:g + 1]
            o_ref[chunk, cols] = (u_ref[chunk, cols] * mixed).astype(o_ref.dtype)


def _gmlp_gate_call(u, v, w_s, b_s_t, rows):
    m_dim, d_a = u.shape
    row_spec = pl.BlockSpec((rows, d_a), lambda m: (m, 0))
    vmem = (6 * _nbytes((rows, d_a), BF16) + 3 * _nbytes((A_GROUPS, CHUNK, CHUNK), F32)
            + 2 * _nbytes((CHUNK, 128), F32))
    return pl.pallas_call(
        functools.partial(_gmlp_gate_kernel, rows=rows),
        grid=(m_dim // rows,),
        in_specs=[row_spec, row_spec,
                  pl.BlockSpec((A_GROUPS, CHUNK, CHUNK), lambda m: (0, 0, 0)),
                  pl.BlockSpec((CHUNK, A_GROUPS), lambda m: (0, 0))],
        out_specs=row_spec,
        out_shape=jax.ShapeDtypeStruct((m_dim, d_a), BF16),
        scratch_shapes=[pltpu.VMEM((A_GROUPS, CHUNK, CHUNK), BF16)],
        compiler_params=_params(1, vmem),
        name="gmlp_spatial_gate",
    )(u, v, w_s, b_s_t)


def _gmlp_gate_row0_kernel(u_ref, v_ref, w00_ref, b0_ref, o_ref):
    o_ref[...] = (u_ref[...] * (w00_ref[...] * v_ref[...] + b0_ref[...])).astype(o_ref.dtype)


def _gmlp_gate_row0_call(u, v, w00, b0):
    s_dim, d_a = u.shape
    full = lambda shape: pl.BlockSpec(shape, lambda i: (0,) * len(shape))
    vmem = 2 * (3 * _nbytes((max(s_dim, 16), d_a), F32) + 2 * _nbytes((8, d_a), F32))
    return pl.pallas_call(
        _gmlp_gate_row0_kernel,
        grid=(1,),
        in_specs=[full((s_dim, d_a)), full((s_dim, d_a)), full((1, d_a)), full((1, d_a))],
        out_specs=full((s_dim, d_a)),
        out_shape=jax.ShapeDtypeStruct((s_dim, d_a), BF16),
        compiler_params=_params(1, vmem),
        name="gmlp_gate_single_token",
    )(u, v, w00, b0)


GATE_ROWS = 16
LOG2_E = 1.4426950408889634


HEAD_TILE = 8
N_PICKED_PAGES = MOBA_TOPK * PAGES_PER_BLOCK


def _moba_kernel(pt_ref, tk_ref, q_ref, k_ref, v_ref, dq_ref, dkn_ref, dvn_ref, *refs, n_blocks, pairs):
    del pt_ref, tk_ref
    n_pages = pairs * N_PICKED_PAGES
    dk_refs, dv_refs = refs[:n_pages], refs[n_pages:2 * n_pages]
    o_ref, do_ref, kb_scr, vt_scr, km_scr, p_scr = refs[2 * n_pages:]
    _moba_prefill(q_ref, k_ref, v_ref, o_ref, kb_scr, vt_scr, km_scr, p_scr, n_blocks)
    first = (pl.program_id(0) * N_HEADS + pl.program_id(1)) * pairs
    for t in range(pairs):
        pair = first + t
        seq, head = pair // N_HEADS, pair % N_HEADS
        picked = slice(t * N_PICKED_PAGES, (t + 1) * N_PICKED_PAGES)
        do_ref[seq, pl.ds(head, 1), :] = _decode_attend(
            dq_ref[seq, pl.ds(head, 1), :], dkn_ref[seq, pl.ds(head, 1), :], dvn_ref[seq, pl.ds(head, 1), :],
            dk_refs[picked], dv_refs[picked], head)


def _moba_prefill(q_ref, k_ref, v_ref, o_ref, kb_scr, vt_scr, km_scr, p_scr, n_blocks):
    contract_last = (((1,), (1,)), ((), ()))
    kb_scr[...] = k_ref[...].astype(BF16)
    vt_scr[...] = v_ref[...].T.astype(BF16)
    km_scr[...] = jnp.zeros_like(km_scr)
    for n in range(n_blocks):
        km_scr[n:n + 1, :] = jnp.mean(k_ref[n * MOBA_BLOCK:(n + 1) * MOBA_BLOCK, :], axis=0, keepdims=True)
    gate_all = lax.dot_general(km_scr[...].astype(BF16), q_ref[...], contract_last,
                               preferred_element_type=F32)
    blk = lax.broadcasted_iota(jnp.int32, (GATE_ROWS, MOBA_BLOCK), 0)
    key = lax.broadcasted_iota(jnp.int32, (MOBA_BLOCK, MOBA_BLOCK), 0)
    qry = lax.broadcasted_iota(jnp.int32, (MOBA_BLOCK, MOBA_BLOCK), 1)

    for i in range(n_blocks):
        rows = slice(i * MOBA_BLOCK, (i + 1) * MOBA_BLOCK)
        q_t = q_ref[rows, :]
        bias = [None] * i
        if i > MOBA_TOPK:
            gate = jnp.where(blk < i, gate_all[:, rows], NEG_INF)
            for n in range(i):
                g_n = gate[n:n + 1, :]
                beats = (gate > g_n) | ((gate == g_n) & (blk < n))
                cnt = jnp.sum(jnp.where(beats, 1.0, 0.0), axis=0, keepdims=True)
                bias[n] = jnp.where(cnt < MOBA_TOPK, 0.0, NEG_INF)
        kv_len = (i + 1) * MOBA_BLOCK
        s_all = lax.dot_general(kb_scr[0:kv_len, :], q_t, contract_last,
                                preferred_element_type=F32) * (ATTN_SCALE * LOG2_E)
        s_blocks = []
        for j in range(i + 1):
            s = s_all[j * MOBA_BLOCK:(j + 1) * MOBA_BLOCK, :]
            if j == i:
                s = jnp.where(key <= qry, s, NEG_INF)
            elif bias[j] is not None:
                s = s + bias[j]
            s_blocks.append(s)
        m = jnp.max(functools.reduce(jnp.maximum, s_blocks), axis=0, keepdims=True)
        l = None
        for j, s in enumerate(s_blocks):
            p = jnp.exp2(s - m)
            p_scr[j * MOBA_BLOCK:(j + 1) * MOBA_BLOCK, :] = p.astype(BF16)
            p_sum = jnp.sum(p, axis=0, keepdims=True)
            l = p_sum if l is None else l + p_sum
        acc = jnp.dot(vt_scr[:, 0:kv_len], p_scr[0:kv_len, :], preferred_element_type=F32)
        o_ref[rows, :] = (acc / l).T.astype(o_ref.dtype)


def _decode_attend(q, k_own, v_own, k_refs, v_refs, head):
    s_own = jnp.sum(q * k_own, axis=-1, keepdims=True) * ATTN_SCALE
    q_rows = jnp.broadcast_to(q, (HEAD_TILE, HEAD_DIM)).astype(BF16)
    rows = PAGE_SIZE * HEAD_TILE
    lane = lax.broadcasted_iota(jnp.int32, (HEAD_TILE, rows), 1)
    mine = (lane & (HEAD_TILE - 1)) == (head & (HEAD_TILE - 1))
    scores = []
    for k_ref in k_refs:
        k2 = k_ref[...].reshape(rows, HEAD_DIM).astype(BF16)
        s = lax.dot_general(q_rows, k2, (((1,), (1,)), ((), ())), preferred_element_type=F32) * ATTN_SCALE
        scores.append(jnp.where(mine, s, NEG_INF))
    m = s_own
    for s in scores:
        m = jnp.maximum(m, jnp.max(s, axis=-1, keepdims=True))
    l = jnp.exp(s_own - m)
    acc = l * v_own
    for s, v_ref in zip(scores, v_refs):
        p = jnp.exp(s - m)
        l = l + jnp.sum(p, axis=-1, keepdims=True)
        acc = acc + jnp.dot(p.astype(BF16), v_ref[...].reshape(rows, HEAD_DIM).astype(BF16), preferred_element_type=F32)
    return (acc / l)[0:1, :]


def _moba_call(page_table_flat, topk_flat, q, k, v, dq, dk_new, dv_new, cache_k, cache_v, pages_per_seq):
    b_dim, l_dim, hd = q.shape
    s_dim = dq.shape[0]
    n_blocks = l_dim // MOBA_BLOCK
    assert n_blocks <= GATE_ROWS and s_dim % b_dim == 0 and N_HEADS & (N_HEADS - 1) == 0
    pairs = s_dim // b_dim
    head_spec = pl.BlockSpec((None, l_dim, HEAD_DIM), lambda b, h, pt, tk: (b, 0, h))
    dec_spec = pl.BlockSpec((s_dim, N_HEADS, HEAD_DIM), lambda b, h, pt, tk: (0, 0, 0))

    def page_spec(t, j, half):
        def index(b, h, pt, tk):
            pair = (b * N_HEADS + h) * pairs + t
            seq, head = pair // N_HEADS, pair % N_HEADS
            block = tk[pair * MOBA_TOPK + j]
            return (pt[seq * pages_per_seq + PAGES_PER_BLOCK * block + half], 0, head // HEAD_TILE, 0)
        return pl.BlockSpec((None, PAGE_SIZE, HEAD_TILE, HEAD_DIM), index)

    page_specs = [page_spec(t, j, half) for t in range(pairs) for j in range(MOBA_TOPK)
                  for half in range(PAGES_PER_BLOCK)]
    n_pages = len(page_specs)
    vmem = (4 * _nbytes((l_dim, HEAD_DIM), F32) + 6 * _nbytes((l_dim, HEAD_DIM), BF16)
            + _nbytes((GATE_ROWS, HEAD_DIM), F32) + _nbytes((l_dim, MOBA_BLOCK), BF16)
            + 2 * _nbytes((l_dim, MOBA_BLOCK), F32)
            + 4 * n_pages * _nbytes((PAGE_SIZE, HEAD_TILE, HEAD_DIM), F32) + 8 * _nbytes((s_dim, N_HEADS, HEAD_DIM), F32))
    return pl.pallas_call(
        functools.partial(_moba_kernel, n_blocks=n_blocks, pairs=pairs),
        grid_spec=pltpu.PrefetchScalarGridSpec(
            num_scalar_prefetch=2,
            grid=(b_dim, N_HEADS),
            in_specs=[head_spec, head_spec, head_spec, dec_spec, dec_spec, dec_spec] + page_specs + page_specs,
            out_specs=[head_spec, dec_spec],
            scratch_shapes=[pltpu.VMEM((l_dim, HEAD_DIM), BF16), pltpu.VMEM((HEAD_DIM, l_dim), BF16),
                            pltpu.VMEM((GATE_ROWS, HEAD_DIM), F32), pltpu.VMEM((l_dim, MOBA_BLOCK), BF16)],
        ),
        out_shape=[jax.ShapeDtypeStruct((b_dim, l_dim, hd), BF16),
                   jax.ShapeDtypeStruct((s_dim, N_HEADS, HEAD_DIM), F32)],
        compiler_params=_params(2, vmem),
        name="moba_attention",
    )(page_table_flat, topk_flat, q, k, v, dq, dk_new, dv_new, *([cache_k] * n_pages), *([cache_v] * n_pages))


def _topk_kernel(q_ref, km_ref, o_ref):
    n_blocks = km_ref.shape[0]
    km = km_ref[...].reshape(n_blocks * N_HEADS, HEAD_DIM).astype(BF16)
    gate = lax.dot_general(q_ref[...].astype(BF16), km, (((1,), (1,)), ((), ())), preferred_element_type=F32)
    lane = lax.broadcasted_iota(jnp.int32, gate.shape, 1)
    head = lax.broadcasted_iota(jnp.int32, gate.shape, 0)
    gate = jnp.where((lane & (N_HEADS - 1)) == head, gate, -jnp.inf)
    lane_f = lane.astype(F32)
    out_lane = lax.broadcasted_iota(jnp.int32, o_ref.shape, 1)
    out = jnp.zeros(o_ref.shape, jnp.int32)
    for t in range(MOBA_TOPK):
        best = jnp.max(gate, axis=-1, keepdims=True)
        arg = jnp.min(jnp.where(gate == best, lane_f, float(gate.shape[1])), axis=-1, keepdims=True)
        out = jnp.where(out_lane == t, arg.astype(jnp.int32) // N_HEADS, out)
        gate = jnp.where(lane_f == arg, -jnp.inf, gate)
    o_ref[...] = out


def _topk_call(q, kmean):
    s_dim = q.shape[0]
    n_blocks = kmean.shape[1]
    assert N_HEADS & (N_HEADS - 1) == 0
    vmem = 2 * (_nbytes((N_HEADS, HEAD_DIM), F32) + _nbytes((n_blocks, N_HEADS, HEAD_DIM), F32)
                + _nbytes((N_HEADS, GATE_LANES), jnp.int32)) + 8 * _nbytes((N_HEADS, n_blocks * N_HEADS), F32)
    return pl.pallas_call(
        _topk_kernel,
        grid=(s_dim,),
        in_specs=[pl.BlockSpec((None, N_HEADS, HEAD_DIM), lambda b: (b, 0, 0)),
                  pl.BlockSpec((None, n_blocks, N_HEADS, HEAD_DIM), lambda b: (b, 0, 0, 0))],
        out_specs=pl.BlockSpec((None, N_HEADS, GATE_LANES), lambda b: (b, 0, 0)),
        out_shape=jax.ShapeDtypeStruct((s_dim, N_HEADS, GATE_LANES), jnp.int32),
        compiler_params=_params(1, vmem),
        name="moba_decode_topk",
    )(q, kmean)


FFN_BN = 512


def _trunk(x, mods, kv_mod, fin_mod, weights, *, bm, rows_per_group, gate_fn, q_dtype, v_dtype):
    (g_mix, g_ffn, w_a_in, b_a_in, ln_a_g, ln_a_b, w_a_out, g_kv, w_k, w_v, w_q, w_o,
     w_ff_gate, w_ff_up, w_ff_down, g_fin) = weights
    depth = g_mix.shape[0]
    n_a_layers = w_a_in.shape[0]
    d_a = w_a_in.shape[2] // 2
    half_bm = bm // 2 if bm % 16 == 0 else bm

    def mmr(x, w, layer, epi, **kwargs):
        out = yield "matmul", dict(kwargs, x=x, w=w, layer=layer, epi=epi, bm=kwargs.get("bm", bm))
        return out

    a_rows = []
    k_new = v_new = None
    for l in range(depth):
        if l == n_a_layers:
            kv_norm = (g_kv,) + tuple(kv_mod)
            k_new, h_kv = yield from mmr(x, w_k, 0, "plain", bm=half_bm, norm=kv_norm, out_dtype=F32, emit_h=True)
            v_new = yield from mmr(h_kv, w_v, 0, "plain", out_dtype=F32)
        sh1, sc1, g1, sh2, sc2, g2 = mods[l]
        mix_norm = (g_mix[l:l + 1], sh1, sc1)
        if l < n_a_layers:
            bias = b_a_in[l:l + 1]
            v_ln, h = yield from mmr(x, w_a_in, l, "gelu_ln", bm=half_bm, norm=mix_norm, bias=bias, emit_h=True,
                                     ln=(ln_a_g[l:l + 1], ln_a_b[l:l + 1]), n_cols=d_a, col_off=d_a, out_dtype=v_dtype)
            u = yield from mmr(h, w_a_in, l, "gelu", bias=bias, n_cols=d_a, out_dtype=BF16)
            a_rows.append(v_ln)
            x = yield from mmr(gate_fn(l, u, v_ln), w_a_out, l, "residual", res=x, gate=g1, out_dtype=F32)
        else:
            j = l - n_a_layers
            q = yield from mmr(x, w_q, j, "plain", norm=mix_norm, out_dtype=q_dtype)
            o = yield "attention", q, k_new, v_new
            x = yield from mmr(o, w_o, j, "residual", res=x, gate=g1, out_dtype=F32)
        h = _norm_call(x, g_ffn[l:l + 1], sh2, sc2, bm=min(bm, 512), rows_per_group=rows_per_group, out_dtype=BF16)
        x = yield "ffn", l, h, x, g2
    f_shift, f_scale = fin_mod
    y = _norm_call(x, g_fin, f_shift, f_scale, bm=min(bm, 512), rows_per_group=rows_per_group, out_dtype=F32)
    return y, k_new, v_new, a_rows


def _send(trunk, value):
    try:
        return trunk.send(value), None
    except StopIteration as done:
        return None, done.value


def kernel(x_prompt, x_sample, c_prompt, c_sample, cache_k, cache_v, page_table, w_ada, b_ada, g_mix, g_ffn, w_a_in, b_a_in, ln_a_g, ln_a_b, w_s, b_s, w_a_out, w_kv_ada, b_kv_ada, g_kv, w_k, w_v, w_q, w_o, w_ff_gate, w_ff_up, w_ff_down, w_fin_ada, b_fin_ada, g_fin):
    batch, seq, d = x_prompt.shape
    n_seq, dec_seq, _ = x_sample.shape
    assert dec_seq == 1 and batch + n_seq <= C_ROWS
    depth = w_ada.shape[0]
    n_pages, page_size, n_heads, head_dim = cache_k.shape
    assert (page_size, n_heads, head_dim) == (PAGE_SIZE, N_HEADS, HEAD_DIM)
    pages_per_seq = page_table.shape[1]
    n_past_blocks = pages_per_seq // PAGES_PER_BLOCK

    c_rows = jnp.concatenate([c_prompt, c_sample, jnp.zeros((C_ROWS - batch - n_seq, d), F32)], axis=0)
    mods = _ada_call(c_rows, w_ada, b_ada[:, None, :])
    kv_mod = _ada_call(c_rows, w_kv_ada[None], b_kv_ada[None, None, :])[0]
    fin_mod = _ada_call(c_rows, w_fin_ada[None], b_fin_ada[None, None, :])[0]

    def prompt_vecs(rows, n):
        return [rows[:batch, i * d:(i + 1) * d][:, None, :] for i in range(n)]

    def sample_vecs(rows, n):
        return [rows[batch:batch + n_seq, i * d:(i + 1) * d][None] for i in range(n)]

    weights = (g_mix, g_ffn, w_a_in, b_a_in, ln_a_g, ln_a_b, w_a_out, g_kv[None], w_k[None], w_v[None], w_q, w_o,
               w_ff_gate, w_ff_up, w_ff_down, g_fin[None])
    b_s_t = jnp.swapaxes(b_s, 1, 2)

    pt_flat = page_table.reshape(-1)
    n_a_layers = w_a_in.shape[0]
    group_w = d // A_GROUPS
    heads = (n_seq, N_HEADS, HEAD_DIM)
    bm_prompt = min(1024, seq)

    ffn_steps = _mmc_steps(batch * seq, w_ff_gate.shape[2], bm_prompt, FFN_BN)
    km_blocks_per_step = -(-(n_seq * n_past_blocks) // (n_a_layers * ffn_steps))

    def prompt_gate(l, u, v_ln):
        return _gmlp_gate_call(u, v_ln, w_s[l], b_s_t[l], rows=512)

    def sample_gate(l, u, v_ln):
        w00 = jnp.repeat(w_s[l, :, 0, 0], group_w)[None, :]
        b0 = jnp.repeat(b_s[l, :, 0], group_w)[None, :]
        return _gmlp_gate_row0_call(u, v_ln, w00, b0)

    prompt = _trunk(
        x_prompt.reshape(batch * seq, d),
        [prompt_vecs(mods[l], 6) for l in range(depth)], prompt_vecs(kv_mod, 2), prompt_vecs(fin_mod, 2),
        weights, bm=bm_prompt, rows_per_group=seq, gate_fn=prompt_gate, q_dtype=BF16, v_dtype=BF16)
    sample = _trunk(
        x_sample.reshape(n_seq, d),
        [sample_vecs(mods[l], 6) for l in range(depth)], sample_vecs(kv_mod, 2), sample_vecs(fin_mod, 2),
        weights, bm=n_seq, rows_per_group=n_seq, gate_fn=sample_gate, q_dtype=F32, v_dtype=F32)

    p_req, s_req = next(prompt), next(sample)
    p_out = s_out = None
    km_parts = []
    kmean = None
    while p_out is None:
        assert p_req[0] == s_req[0]
        if p_req[0] == "matmul":
            main, side = dict(p_req[1]), s_req[1]
            assert main["w"] is side["w"] and (main["layer"], main["epi"]) == (side["layer"], side["epi"])
            outs = _mmr_call(main.pop("x"), main.pop("w"), main.pop("layer"), main.pop("epi"), bn=min(1024, d),
                             rows_per_group=seq, side={key: side.get(key) for key in
                                                       ("x", "norm", "res", "gate", "out_dtype")}, **main)
            to_p, to_s = (outs[:2], outs[2:]) if main.get("emit_h") else outs
        elif p_req[0] == "ffn":
            (_, l, h_p, x_p, g2_p), (_, _, h_s, x_s, g2_s) = p_req, s_req
            km_job = None
            if l < n_a_layers:
                km_job = (pt_flat, cache_k, l * ffn_steps * km_blocks_per_step, km_blocks_per_step)
            outs = _mmc_call(h_p, [w_ff_gate, w_ff_up], l, "swiglu", bm=bm_prompt, bn=FFN_BN, side=(h_s,), kmean=km_job)
            a_p, a_s = outs[0], outs[-1]
            if km_job is not None:
                km_parts.append(outs[1])
            to_p, to_s = _mmc_call(a_p, [w_ff_down], l, "residual", bm=bm_prompt // 2, bn=FFN_BN, out_dtype=F32,
                                   res=x_p, gate=g2_p, rows_per_group=seq, side=(a_s, x_s, g2_s))
        else:
            (_, q_p, k_p, v_p), (_, q_s, k_s, v_s) = p_req, s_req
            if kmean is None:
                kmean = jnp.concatenate(km_parts, axis=0)[:n_seq * n_past_blocks]
                kmean = kmean.reshape(n_seq, n_past_blocks, N_HEADS, HEAD_DIM)
            q_s = q_s.reshape(heads)
            topk = _topk_call(q_s, kmean)[:, :, :MOBA_TOPK].reshape(-1)
            o_p, o_s = _moba_call(pt_flat, topk, q_p.reshape(batch, seq, d), k_p.reshape(batch, seq, d),
                                  v_p.reshape(batch, seq, d), q_s, k_s.reshape(heads), v_s.reshape(heads),
                                  cache_k, cache_v, pages_per_seq)
            to_p, to_s = o_p.reshape(batch * seq, d), o_s.reshape(n_seq, d).astype(BF16)
        p_req, p_out = _send(prompt, to_p)
        s_req, s_out = _send(sample, to_s)
    y_p, k_p, v_p, _ = p_out
    y_s, k_s, v_s, a_rows = s_out

    state_a_v = jnp.stack(a_rows, axis=0).reshape(len(a_rows), n_seq, 1, d)
    return (y_p.reshape(batch, seq, d), y_s.reshape(n_seq, 1, d),
            k_p.reshape(batch, seq, N_HEADS, HEAD_DIM), v_p.reshape(batch, seq, N_HEADS, HEAD_DIM),
            k_s.reshape(n_seq, 1, N_HEADS, HEAD_DIM), v_s.reshape(n_seq, 1, N_HEADS, HEAD_DIM),
            state_a_v)
```
